```python
import jax, jax.numpy as jnp
from jax import lax
import numpy as np

D_MODEL = 2048
BATCH = 4
SEQ = 2048
DEPTH = 1

D_A = D_MODEL
HGRN_HEAD_DIM = 128
N_HGRN_HEADS = D_A // HGRN_HEAD_DIM
D_B = D_MODEL
CONV_GROUP_DIM = 128
N_CONV_GROUPS = D_B // CONV_GROUP_DIM
CONV_WIDTH = 3
D_MIX = D_A + D_B
CHUNK = 64
PLE_DIM = 256
EPS = 1e-6
IN_SIZES = (D_A, D_A, D_A, D_A, D_A, D_B, D_B, D_B, D_B)
D_IN_TOTAL = sum(IN_SIZES)
IN_SPLITS = tuple(int(s) for s in np.cumsum(IN_SIZES)[:-1])

kernel_name = 'hymba_hgrn2_shortconv_bidir_encoder'


def _rmsnorm(x, w):
    xf = x.astype(jnp.float32)
    xf = xf * lax.rsqrt(jnp.mean(xf * xf, axis=-1, keepdims=True) + EPS)
    return (xf * w.astype(jnp.float32)).astype(x.dtype)


def _head_rmsnorm(o, w):
    b, s, _ = o.shape
    of = o.astype(jnp.float32).reshape(b, s, N_HGRN_HEADS, HGRN_HEAD_DIM)
    of = of * lax.rsqrt(jnp.mean(of * of, axis=-1, keepdims=True) + EPS)
    return (of.reshape(b, s, D_A) * w.astype(jnp.float32)).astype(o.dtype)


def _to_chunks(t):
    b, s, _ = t.shape
    return t.reshape(b, s // CHUNK, CHUNK, N_HGRN_HEADS, HGRN_HEAD_DIM).transpose(0, 3, 1, 2, 4)


def _from_chunks(t):
    b, h, n, c, d = t.shape
    return t.transpose(0, 2, 3, 1, 4).reshape(b, n * c, h * d)


def _gla_chunked(q, k, v, log_f):
    b = jnp.cumsum(log_f, axis=-2)
    b_mid = b[..., CHUNK // 2 - 1:CHUNK // 2, :]
    b_last = b[..., -1:, :]
    q_rel = q * jnp.exp(b - b_mid)
    k_rel = k * jnp.exp(b_mid - b)
    scores = jnp.einsum('bhntk,bhnsk->bhnts', q_rel, k_rel)
    mask = jnp.tril(jnp.ones((CHUNK, CHUNK), dtype=bool))
    scores = jnp.where(mask, scores, 0.0)
    o_intra = jnp.einsum('bhnts,bhnsv->bhntv', scores, v)
    q_in = q * jnp.exp(b)
    k_st = k * jnp.exp(b_last - b)
    kv = jnp.einsum('bhnsk,bhnsv->bhnkv', k_st, v)
    decay = jnp.exp(b_last[..., 0, :])

    def step(state, inp):
        d_n, kv_n = inp
        return d_n[..., None] * state + kv_n, state

    bsz, h = q.shape[0], q.shape[1]
    s0 = jnp.zeros((bsz, h, HGRN_HEAD_DIM, HGRN_HEAD_DIM), q.dtype)
    _, s_prev = lax.scan(step, s0, (jnp.moveaxis(decay, 2, 0), jnp.moveaxis(kv, 2, 0)))
    s_prev = jnp.moveaxis(s_prev, 0, 2)
    o_inter = jnp.einsum('bhntk,bhnkv->bhntv', q_in, s_prev)
    return o_intra + o_inter


def _hgrn2_direction(q, v, f_logit, lb, reverse):
    f = lb + (1.0 - lb) * jax.nn.sigmoid(f_logit)
    log_f = jnp.log(f)
    k = 1.0 - f
    if reverse:
        q, v, k, log_f = (jnp.flip(t, axis=1) for t in (q, v, k, log_f))
    o = _from_chunks(_gla_chunked(_to_chunks(q), _to_chunks(k), _to_chunks(v), _to_chunks(log_f)))
    if reverse:
        o = jnp.flip(o, axis=1)
    return o


def _hgrn2_bidirectional(q, v, f_fwd, f_bwd, lb):
    dt = q.dtype
    q32, v32 = q.astype(jnp.float32), v.astype(jnp.float32)
    o = (_hgrn2_direction(q32, v32, f_fwd.astype(jnp.float32), lb[0], False)
         + _hgrn2_direction(q32, v32, f_bwd.astype(jnp.float32), lb[1], True))
    return o.astype(dt)


def _short_conv(u, w):
    pad = CONV_WIDTH // 2
    return lax.conv_general_dilated(
        u, w[:, None, :].astype(u.dtype), window_strides=(1,), padding=((pad, pad),),
        dimension_numbers=('NWC', 'WIO', 'NWC'), feature_group_count=D_B)


def setup_inputs(seed: int = 0) -> dict:
    key = jax.random.key(seed)
    ks = jax.random.split(key, 13)
    nrm = jax.random.normal
    return {
        'x': nrm(ks[0], (BATCH, SEQ, D_MODEL), jnp.float32),
        'p': nrm(ks[1], (DEPTH, BATCH, SEQ, PLE_DIM), jnp.float32),
        'norm_w': 1.0 + 0.02 * nrm(ks[2], (DEPTH, D_MODEL), jnp.float32),
        'w_in': nrm(ks[3], (DEPTH, D_MODEL, D_IN_TOTAL), jnp.float32) * D_MODEL ** -0.5,
        'lb_theta': 0.1 * nrm(ks[4], (2, DEPTH + 1, D_A), jnp.float32),
        'hgrn_norm_w': 1.0 + 0.02 * nrm(ks[5], (DEPTH, D_A), jnp.float32),
        'conv_w': nrm(ks[6], (DEPTH, CONV_WIDTH, D_B), jnp.float32) * CONV_WIDTH ** -0.5,
        'conv_norm_w': 1.0 + 0.02 * nrm(ks[7], (DEPTH, D_B), jnp.float32),
        'w_out': nrm(ks[8], (DEPTH, D_MIX, D_MODEL), jnp.float32) * D_MIX ** -0.5,
        'w_ple': nrm(ks[9], (DEPTH, PLE_DIM, D_MODEL), jnp.float32) * PLE_DIM ** -0.5,
        'w_ple_gate': nrm(ks[10], (DEPTH, D_MODEL, D_MODEL), jnp.float32) * D_MODEL ** -0.5,
        'final_norm_w': 1.0 + 0.02 * nrm(ks[11], (D_MODEL,), jnp.float32),
    }


def reference(x, p, norm_w, w_in, lb_theta, hgrn_norm_w, conv_w, conv_norm_w,
              w_out, w_ple, w_ple_gate, final_norm_w):
    h = x
    lb_all = jnp.cumsum(jax.nn.softmax(lb_theta.astype(jnp.float32), axis=1), axis=1)
    for i in range(DEPTH):
        hn = _rmsnorm(h, norm_w[i])
        proj = hn @ w_in[i]
        q, v, f_fwd, f_bwd, z_a, b_gate, c_gate, h_b, z_b = jnp.split(proj, IN_SPLITS, axis=-1)
        o_a = _hgrn2_bidirectional(jax.nn.silu(q), v, f_fwd, f_bwd, lb_all[:, i])
        o_a = _head_rmsnorm(o_a, hgrn_norm_w[i]) * jax.nn.silu(z_a)
        y_b = b_gate * _short_conv(c_gate * h_b, conv_w[i])
        o_b = _rmsnorm(y_b, conv_norm_w[i]) * jax.nn.silu(z_b)
        h = h + jnp.concatenate([o_a, o_b], axis=-1) @ w_out[i]
        gate = jax.nn.sigmoid(h @ w_ple_gate[i])
        h = h + (p[i] @ w_ple[i]) * gate
    return _rmsnorm(h, final_norm_w)
```

```python
import functools

import jax
import jax.numpy as jnp
from jax import lax
from jax.experimental import pallas as pl
from jax.experimental.pallas import tpu as pltpu

EPS = 1e-6
LANES = 128
CHUNK = 64
VMEM_LIMIT_BYTES = 56 * 1024 * 1024

F32 = jnp.float32
BF16 = jnp.bfloat16


def _sigmoid(x):
    return 1.0 / (1.0 + jnp.exp(-x))


def _dot(a, b, dims=((1,), (0,))):
    return lax.dot_general(a, b, (dims, ((), ())), preferred_element_type=F32)


def _rmsnorm_kernel(x_ref, w_ref, o_ref):
    x = x_ref[...]
    ms = jnp.mean(x * x, axis=-1, keepdims=True)
    o_ref[...] = (x * lax.rsqrt(ms + EPS) * w_ref[...]).astype(o_ref.dtype)


def _rmsnorm(x2d, w, tm=512):
    t, d = x2d.shape
    return pl.pallas_call(
        _rmsnorm_kernel,
        grid=(t // tm,),
        in_specs=[pl.BlockSpec((tm, d), lambda i: (i, 0)),
                  pl.BlockSpec((1, d), lambda i: (0, 0))],
        out_specs=pl.BlockSpec((tm, d), lambda i: (i, 0)),
        out_shape=jax.ShapeDtypeStruct((t, d), BF16),
        name="rmsnorm",
    )(x2d, w.reshape(1, d))


def _inproj_kernel(h_ref, w_ref, o_ref, wbf_ref):
    @pl.when(pl.program_id(1) == 0)
    def _():
        wbf_ref[...] = w_ref[...].astype(BF16)

    acc = jnp.dot(h_ref[...], wbf_ref[...], preferred_element_type=F32)
    for g in range(o_ref.shape[0]):
        o_ref[g] = acc[:, g * LANES:(g + 1) * LANES].astype(o_ref.dtype)


def _inproj(hn, w, tm=1024, tn=1024):
    t, d = hn.shape
    n = w.shape[1]
    gpb = tn // LANES
    return pl.pallas_call(
        _inproj_kernel,
        grid=(n // tn, t // tm),
        in_specs=[pl.BlockSpec((tm, d), lambda j, i: (i, 0)),
                  pl.BlockSpec((d, tn), lambda j, i: (0, j))],
        out_specs=pl.BlockSpec((gpb, tm, LANES), lambda j, i: (j, i, 0)),
        out_shape=jax.ShapeDtypeStruct((n // LANES, t, LANES), BF16),
        scratch_shapes=[pltpu.VMEM((d, tn), BF16)],
        compiler_params=pltpu.CompilerParams(
            dimension_semantics=("arbitrary", "arbitrary"),
            vmem_limit_bytes=VMEM_LIMIT_BYTES),
        name="inproj",
    )(hn, w)


def _hgrn_kernel(q_ref, v_ref, ff_ref, fb_ref, z_ref, theta_ref, nw_ref, o_ref,
                 acc_ref, qin_ref, kv_ref, dec_ref, sprev_ref):
    seq = q_ref.shape[1]
    n_chunks = seq // CHUNK
    head = pl.program_id(1)
    c = CHUNK
    mid = c // 2 - 1

    row = lax.broadcasted_iota(jnp.int32, (c, c), 0)
    col = lax.broadcasted_iota(jnp.int32, (c, c), 1)
    masks = (col <= row, col >= row)
    row2 = lax.broadcasted_iota(jnp.int32, (c, 2 * c), 0)
    col2 = lax.broadcasted_iota(jnp.int32, (c, 2 * c), 1) % c
    tris = ((col2 <= row2).astype(BF16), (col2 >= row2).astype(BF16))
    f_refs = (ff_ref, fb_ref)
    mid_rows = (mid, c - 1 - mid)
    last_rows = (c - 1, 0)

    lbs = []
    for d in range(2):
        t0 = theta_ref[d, 0, pl.ds(head, 1), :]
        t1 = theta_ref[d, 1, pl.ds(head, 1), :]
        m = jnp.maximum(t0, t1)
        e0 = jnp.exp(t0 - m)
        e1 = jnp.exp(t1 - m)
        lbs.append(e0 / (e0 + e1))

    def intra(n, carry):
        r = pl.ds(pl.multiple_of(n * c, c), c)
        q = q_ref[0, r, :].astype(F32)
        v = v_ref[0, r, :]
        qs = q * _sigmoid(q)
        p = jnp.zeros((c, c), F32)
        for d in range(2):
            x = f_refs[d][0, r, :].astype(F32)
            f = lbs[d] + (1.0 - lbs[d]) * _sigmoid(x)
            lf = jnp.log(f)
            k = 1.0 - f
            hi = lf.astype(BF16)
            lo = (lf - hi.astype(F32)).astype(BF16)
            b = _dot(tris[d], jnp.concatenate([hi, lo], axis=0))
            b_mid = b[mid_rows[d]:mid_rows[d] + 1, :]
            b_last = b[last_rows[d]:last_rows[d] + 1, :]
            q_rel = qs * jnp.exp(b - b_mid)
            k_rel = k * jnp.exp(b_mid - b)
            s = _dot(q_rel.astype(BF16), k_rel.astype(BF16), ((1,), (1,)))
            p = p + jnp.where(masks[d], s, 0.0)
            q_in = q_rel * jnp.exp(b_mid)
            k_st = k_rel * jnp.exp(b_last - b_mid)
            qin_ref[d, r, :] = q_in.astype(BF16)
            kv_ref[d, n] = _dot(v, k_st.astype(BF16), ((0,), (0,)))
            dec_ref[d, n] = jnp.broadcast_to(jnp.exp(b_last), (8, LANES))
        acc_ref[r, :] = _dot(p.astype(BF16), v)
        return carry

    lax.fori_loop(0, n_chunks, intra, 0, unroll=2)

    def scan_fwd(n, st):
        sprev_ref[n, :, 0:LANES] = st.astype(BF16)
        return st * dec_ref[0, n][0:1, :] + kv_ref[0, n]

    def scan_bwd(i, st):
        n = n_chunks - 1 - i
        sprev_ref[n, :, LANES:2 * LANES] = st.astype(BF16)
        return st * dec_ref[1, n][0:1, :] + kv_ref[1, n]

    zero = jnp.zeros((LANES, LANES), F32)
    lax.fori_loop(0, n_chunks, scan_fwd, zero)
    lax.fori_loop(0, n_chunks, scan_bwd, zero)

    nw = nw_ref[pl.ds(head, 1), :]

    def finish(n, carry):
        r = pl.ds(pl.multiple_of(n * c, c), c)
        qin = jnp.concatenate([qin_ref[0, r, :], qin_ref[1, r, :]], axis=1)
        o = acc_ref[r, :] + _dot(qin, sprev_ref[n], ((1,), (1,)))
        ms = jnp.mean(o * o, axis=-1, keepdims=True)
        z = z_ref[0, r, :].astype(F32)
        o_ref[r, :] = (o * lax.rsqrt(ms + EPS) * nw * (z * _sigmoid(z))).astype(o_ref.dtype)
        return carry

    lax.fori_loop(0, n_chunks, finish, 0, unroll=2)


def _hgrn(proj, theta, nw, batch, seq, n_heads):
    t = batch * seq
    n_chunks = seq // CHUNK

    def spec(slot):
        return pl.BlockSpec((1, seq, LANES), lambda b, h, slot=slot: (slot * n_heads + h, b, 0))

    return pl.pallas_call(
        _hgrn_kernel,
        grid=(batch, n_heads),
        in_specs=[spec(0), spec(1), spec(2), spec(3), spec(4),
                  pl.BlockSpec(theta.shape, lambda b, h: (0, 0, 0, 0)),
                  pl.BlockSpec(nw.shape, lambda b, h: (0, 0))],
        out_specs=pl.BlockSpec((seq, LANES), lambda b, h: (b, h)),
        out_shape=jax.ShapeDtypeStruct((t, n_heads * LANES), BF16),
        scratch_shapes=[
            pltpu.VMEM((seq, LANES), F32),
            pltpu.VMEM((2, seq, LANES), BF16),
            pltpu.VMEM((2, n_chunks, LANES, LANES), F32),
            pltpu.VMEM((2, n_chunks, 8, LANES), F32),
            pltpu.VMEM((n_chunks, LANES, 2 * LANES), BF16),
        ],
        compiler_params=pltpu.CompilerParams(
            dimension_semantics=("arbitrary", "arbitrary"),
            vmem_limit_bytes=VMEM_LIMIT_BYTES),
        name="hgrn2",
    )(proj, proj, proj, proj, proj, theta, nw)


def _conv_kernel(bg_ref, c_ref, h_ref, z_ref, cp_ref, hp_ref, cn_ref, hn_ref,
                 cw_ref, nw_ref, o_ref, y_ref, *, blocks_per_seq):
    n_groups, tm, _ = c_ref.shape
    halo = cp_ref.shape[1]
    i = pl.program_id(0)
    first = (i % blocks_per_seq) == 0
    last = (i % blocks_per_seq) == blocks_per_seq - 1
    row = lax.broadcasted_iota(jnp.int32, (tm, LANES), 0)

    ss = jnp.zeros((tm, LANES), F32)
    for g in range(n_groups):
        u = c_ref[g].astype(F32) * h_ref[g].astype(F32)
        up = cp_ref[g, halo - 1:halo, :].astype(F32) * hp_ref[g, halo - 1:halo, :].astype(F32)
        un = cn_ref[g, 0:1, :].astype(F32) * hn_ref[g, 0:1, :].astype(F32)
        up = jnp.where(first, 0.0, up)
        un = jnp.where(last, 0.0, un)
        u_prev = jnp.where(row == 0, up, pltpu.roll(u, 1, axis=0))
        u_next = jnp.where(row == tm - 1, un, pltpu.roll(u, tm - 1, axis=0))
        conv = (cw_ref[0, g:g + 1, :] * u_prev + cw_ref[1, g:g + 1, :] * u
                + cw_ref[2, g:g + 1, :] * u_next)
        y = bg_ref[g].astype(F32) * conv
        y_ref[g] = y
        ss = ss + y * y
    ms = jnp.sum(ss, axis=-1, keepdims=True) * (1.0 / (n_groups * LANES))
    inv = lax.rsqrt(ms + EPS)
    for g in range(n_groups):
        z = z_ref[g].astype(F32)
        o_ref[:, g * LANES:(g + 1) * LANES] = (
            y_ref[g] * inv * nw_ref[g:g + 1, :] * (z * _sigmoid(z))).astype(o_ref.dtype)


def _conv(proj, cw, nw, seq, n_groups, first_slot, tm=256, halo=16):
    t = proj.shape[1]
    blocks_per_seq = seq // tm
    hb = tm // halo
    n_halo = t // halo

    def spec(slot):
        return pl.BlockSpec((n_groups, tm, LANES), lambda i, slot=slot: (slot, i, 0))

    def prev_spec(slot):
        return pl.BlockSpec((n_groups, halo, LANES),
                            lambda i, slot=slot: (slot, jnp.maximum(i * hb - 1, 0), 0))

    def next_spec(slot):
        return pl.BlockSpec((n_groups, halo, LANES),
                            lambda i, slot=slot: (slot, jnp.minimum((i + 1) * hb, n_halo - 1), 0))

    s_b, s_c, s_h, s_z = (first_slot + k for k in range(4))
    return pl.pallas_call(
        functools.partial(_conv_kernel, blocks_per_seq=blocks_per_seq),
        grid=(t // tm,),
        in_specs=[spec(s_b), spec(s_c), spec(s_h), spec(s_z),
                  prev_spec(s_c), prev_spec(s_h), next_spec(s_c), next_spec(s_h),
                  pl.BlockSpec(cw.shape, lambda i: (0, 0, 0)),
                  pl.BlockSpec(nw.shape, lambda i: (0, 0))],
        out_specs=pl.BlockSpec((tm, n_groups * LANES), lambda i: (i, 0)),
        out_shape=jax.ShapeDtypeStruct((t, n_groups * LANES), BF16),
        scratch_shapes=[pltpu.VMEM((n_groups, tm, LANES), F32)],
        compiler_params=pltpu.CompilerParams(
            dimension_semantics=("arbitrary",),
            vmem_limit_bytes=VMEM_LIMIT_BYTES),
        name="shortconv",
    )(proj, proj, proj, proj, proj, proj, proj, proj, cw, nw)


def _out_kernel(oa_ref, ob_ref, x_ref, p_ref, wo_ref, wg_ref, wp_ref, fw_ref, o_ref):
    d_a = oa_ref.shape[1]
    h1 = (x_ref[...]
          + jnp.dot(oa_ref[...], wo_ref[0:d_a, :], preferred_element_type=F32)
          + jnp.dot(ob_ref[...], wo_ref[d_a:, :], preferred_element_type=F32))
    gate = _sigmoid(jnp.dot(h1.astype(BF16), wg_ref[...], preferred_element_type=F32))
    ple = jnp.dot(p_ref[...].astype(BF16), wp_ref[...], preferred_element_type=F32)
    h2 = h1 + ple * gate
    ms = jnp.mean(h2 * h2, axis=-1, keepdims=True)
    o_ref[...] = h2 * lax.rsqrt(ms + EPS) * fw_ref[...]


def _out(oa, ob, x2d, p2d, wo, wg, wp, fw, tm=256):
    t, d = x2d.shape

    def resident(shape):
        return pl.BlockSpec(shape, lambda i: (0, 0), pipeline_mode=pl.Buffered(1))

    return pl.pallas_call(
        _out_kernel,
        grid=(t // tm,),
        in_specs=[pl.BlockSpec((tm, oa.shape[1]), lambda i: (i, 0)),
                  pl.BlockSpec((tm, ob.shape[1]), lambda i: (i, 0)),
                  pl.BlockSpec((tm, d), lambda i: (i, 0)),
                  pl.BlockSpec((tm, p2d.shape[1]), lambda i: (i, 0)),
                  resident(wo.shape), resident(wg.shape), resident(wp.shape),
                  resident((1, d))],
        out_specs=pl.BlockSpec((tm, d), lambda i: (i, 0)),
        out_shape=jax.ShapeDtypeStruct((t, d), F32),
        compiler_params=pltpu.CompilerParams(
            dimension_semantics=("arbitrary",),
            vmem_limit_bytes=VMEM_LIMIT_BYTES),
        name="outproj",
    )(oa, ob, x2d, p2d, wo, wg, wp, fw.reshape(1, d))


def kernel(x, p, norm_w, w_in, lb_theta, hgrn_norm_w, conv_w, conv_norm_w, w_out, w_ple,
           w_ple_gate, final_norm_w):
    batch, seq, d = x.shape
    depth = p.shape[0]
    assert depth == 1 and lb_theta.shape[1] == 2
    t = batch * seq
    n_heads = d // LANES
    n_groups = d // LANES

    x2d = x.reshape(t, d)
    hn = _rmsnorm(x2d, norm_w[0])
    proj = _inproj(hn, w_in[0])

    theta = lb_theta.reshape(2, 2, n_heads, LANES)
    o_a = _hgrn(proj, theta, hgrn_norm_w[0].reshape(n_heads, LANES), batch, seq, n_heads)
    o_b = _conv(proj, conv_w[0].reshape(3, n_groups, LANES),
                conv_norm_w[0].reshape(n_groups, LANES), seq, n_groups, first_slot=5)

    out = _out(o_a, o_b, x2d, p[0].reshape(t, -1),
               w_out[0].astype(BF16), w_ple_gate[0].astype(BF16), w_ple[0].astype(BF16),
               final_norm_w)
    return out.reshape(batch, seq, d)
```

```python
import functools

import jax
import jax.numpy as jnp
from jax import lax
from jax.experimental import pallas as pl
from jax.experimental.pallas import tpu as pltpu

EPS = 1e-6
LANES = 128
CHUNK = 64
GROUP = 8
VMEM_LIMIT_BYTES = 56 * 1024 * 1024

F32 = jnp.float32
BF16 = jnp.bfloat16


def _sigmoid(x):
    return 1.0 / (1.0 + jnp.exp(-x))


def _dot(a, b, dims=((1,), (0,))):
    return lax.dot_general(a, b, (dims, ((), ())), preferred_element_type=F32)


def _rmsnorm_kernel(x_ref, w_ref, o_ref):
    x = x_ref[...]
    ms = jnp.mean(x * x, axis=-1, keepdims=True)
    o_ref[...] = (x * lax.rsqrt(ms + EPS) * w_ref[...]).astype(o_ref.dtype)


def _rmsnorm(x2d, w, tm=512):
    t, d = x2d.shape
    return pl.pallas_call(
        _rmsnorm_kernel,
        grid=(t // tm,),
        in_specs=[pl.BlockSpec((tm, d), lambda i: (i, 0)),
                  pl.BlockSpec((1, d), lambda i: (0, 0))],
        out_specs=pl.BlockSpec((tm, d), lambda i: (i, 0)),
        out_shape=jax.ShapeDtypeStruct((t, d), BF16),
        name="rmsnorm",
    )(x2d, w.reshape(1, d))


def _inproj_kernel(h_ref, w_ref, o_ref, wbf_ref):
    @pl.when(pl.program_id(1) == 0)
    def _():
        wbf_ref[...] = w_ref[...].astype(BF16)

    acc = jnp.dot(h_ref[...], wbf_ref[...], preferred_element_type=F32)
    for g in range(o_ref.shape[0]):
        o_ref[g] = acc[:, g * LANES:(g + 1) * LANES].astype(o_ref.dtype)


def _inproj(hn, w, tm=1024, tn=1024):
    t, d = hn.shape
    n = w.shape[1]
    gpb = tn // LANES
    return pl.pallas_call(
        _inproj_kernel,
        grid=(n // tn, t // tm),
        in_specs=[pl.BlockSpec((tm, d), lambda j, i: (i, 0)),
                  pl.BlockSpec((d, tn), lambda j, i: (0, j))],
        out_specs=pl.BlockSpec((gpb, tm, LANES), lambda j, i: (j, i, 0)),
        out_shape=jax.ShapeDtypeStruct((n // LANES, t, LANES), BF16),
        scratch_shapes=[pltpu.VMEM((d, tn), BF16)],
        compiler_params=pltpu.CompilerParams(
            dimension_semantics=("arbitrary", "arbitrary"),
            vmem_limit_bytes=VMEM_LIMIT_BYTES),
        name="inproj",
    )(hn, w)


def _hgrn_kernel(q_ref, v_ref, ff_ref, fb_ref, z_ref, theta_ref, nw_ref, o_ref,
                 acc_ref, qin_ref, kv_ref, dec_ref, sprev_ref):
    seq = q_ref.shape[1]
    n_chunks = seq // CHUNK
    head = pl.program_id(1)
    c = CHUNK
    mid = c // 2 - 1

    row = lax.broadcasted_iota(jnp.int32, (c, c), 0)
    col = lax.broadcasted_iota(jnp.int32, (c, c), 1)
    masks = (col <= row, col >= row)
    row2 = lax.broadcasted_iota(jnp.int32, (c, 2 * c), 0)
    col2 = lax.broadcasted_iota(jnp.int32, (c, 2 * c), 1) % c
    tris = ((col2 <= row2).astype(BF16), (col2 >= row2).astype(BF16))
    f_refs = (ff_ref, fb_ref)
    mid_rows = (mid, c - 1 - mid)
    last_rows = (c - 1, 0)

    lbs = []
    for d in range(2):
        t0 = theta_ref[d, 0, pl.ds(head, 1), :]
        t1 = theta_ref[d, 1, pl.ds(head, 1), :]
        m = jnp.maximum(t0, t1)
        e0 = jnp.exp(t0 - m)
        e1 = jnp.exp(t1 - m)
        lbs.append(e0 / (e0 + e1))

    gsz = GROUP * c
    n_groups = n_chunks // GROUP

    def rows(a, j):
        return a[j * c:(j + 1) * c]

    def intra(g, carry):
        r = pl.ds(pl.multiple_of(g * gsz, gsz), gsz)
        q = q_ref[0, r, :].astype(F32)
        v = v_ref[0, r, :]
        qs = q * _sigmoid(q)
        ks, bs = [], []
        for d in range(2):
            x = f_refs[d][0, r, :].astype(F32)
            f = lbs[d] + (1.0 - lbs[d]) * _sigmoid(x)
            lf = jnp.log(f)
            ks.append(1.0 - f)
            hi = lf.astype(BF16)
            lo = (lf - hi.astype(F32)).astype(BF16)
            bs.append([_dot(tris[d], jnp.concatenate([rows(hi, j), rows(lo, j)], axis=0))
                       for j in range(GROUP)])
        ss = ([], [])
        for j in range(GROUP):
            n = g * GROUP + j
            rj = pl.ds(pl.multiple_of(n * c, c), c)
            for d in range(2):
                b = bs[d][j]
                b_mid = b[mid_rows[d]:mid_rows[d] + 1, :]
                b_last = b[last_rows[d]:last_rows[d] + 1, :]
                q_rel = rows(qs, j) * jnp.exp(b - b_mid)
                k_rel = rows(ks[d], j) * jnp.exp(b_mid - b)
                ss[d].append(_dot(q_rel.astype(BF16), k_rel.astype(BF16), ((1,), (1,))))
                q_in = q_rel * jnp.exp(b_mid)
                k_st = k_rel * jnp.exp(b_last - b_mid)
                qin_ref[d, rj, :] = q_in.astype(BF16)
                kv_ref[d, n] = _dot(rows(v, j), k_st.astype(BF16), ((0,), (0,)))
                dec_ref[d, n] = jnp.broadcast_to(jnp.exp(b_last), (8, LANES))
        for j in range(GROUP):
            rj = pl.ds(pl.multiple_of((g * GROUP + j) * c, c), c)
            p = jnp.where(masks[0], ss[0][j], 0.0) + jnp.where(masks[1], ss[1][j], 0.0)
            acc_ref[rj, :] = _dot(p.astype(BF16), rows(v, j))
        return carry

    lax.fori_loop(0, n_groups, intra, 0)

    def scan_fwd(n, st):
        sprev_ref[n, :, 0:LANES] = st.astype(BF16)
        return st * dec_ref[0, n][0:1, :] + kv_ref[0, n]

    def scan_bwd(i, st):
        n = n_chunks - 1 - i
        sprev_ref[n, :, LANES:2 * LANES] = st.astype(BF16)
        return st * dec_ref[1, n][0:1, :] + kv_ref[1, n]

    zero = jnp.zeros((LANES, LANES), F32)
    lax.fori_loop(0, n_chunks, scan_fwd, zero)
    lax.fori_loop(0, n_chunks, scan_bwd, zero)

    nw = nw_ref[pl.ds(head, 1), :]

    def finish(g, carry):
        r = pl.ds(pl.multiple_of(g * gsz, gsz), gsz)
        inter = []
        for j in range(GROUP):
            n = g * GROUP + j
            rj = pl.ds(pl.multiple_of(n * c, c), c)
            qin = jnp.concatenate([qin_ref[0, rj, :], qin_ref[1, rj, :]], axis=1)
            inter.append(_dot(qin, sprev_ref[n], ((1,), (1,))))
        o = acc_ref[r, :] + jnp.concatenate(inter, axis=0)
        ms = jnp.mean(o * o, axis=-1, keepdims=True)
        z = z_ref[0, r, :].astype(F32)
        o_ref[r, :] = (o * lax.rsqrt(ms + EPS) * nw * (z * _sigmoid(z))).astype(o_ref.dtype)
        return carry

    lax.fori_loop(0, n_groups, finish, 0)


def _hgrn(proj, theta, nw, batch, seq, n_heads):
    t = batch * seq
    n_chunks = seq // CHUNK

    def spec(slot):
        return pl.BlockSpec((1, seq, LANES), lambda b, h, slot=slot: (slot * n_heads + h, b, 0))

    return pl.pallas_call(
        _hgrn_kernel,
        grid=(batch, n_heads),
        in_specs=[spec(0), spec(1), spec(2), spec(3), spec(4),
                  pl.BlockSpec(theta.shape, lambda b, h: (0, 0, 0, 0)),
                  pl.BlockSpec(nw.shape, lambda b, h: (0, 0))],
        out_specs=pl.BlockSpec((seq, LANES), lambda b, h: (b, h)),
        out_shape=jax.ShapeDtypeStruct((t, n_heads * LANES), BF16),
        scratch_shapes=[
            pltpu.VMEM((seq, LANES), F32),
            pltpu.VMEM((2, seq, LANES), BF16),
            pltpu.VMEM((2, n_chunks, LANES, LANES), F32),
            pltpu.VMEM((2, n_chunks, 8, LANES), F32),
            pltpu.VMEM((n_chunks, LANES, 2 * LANES), BF16),
        ],
        compiler_params=pltpu.CompilerParams(
            dimension_semantics=("arbitrary", "arbitrary"),
            vmem_limit_bytes=VMEM_LIMIT_BYTES),
        name="hgrn2",
    )(proj, proj, proj, proj, proj, theta, nw)


def _conv_kernel(bg_ref, c_ref, h_ref, z_ref, cp_ref, hp_ref, cn_ref, hn_ref,
                 cw_ref, nw_ref, o_ref, y_ref, *, blocks_per_seq):
    n_groups, tm, _ = c_ref.shape
    halo = cp_ref.shape[1]
    i = pl.program_id(0)
    first = (i % blocks_per_seq) == 0
    last = (i % blocks_per_seq) == blocks_per_seq - 1
    row = lax.broadcasted_iota(jnp.int32, (tm, LANES), 0)

    ss = jnp.zeros((tm, LANES), F32)
    for g in range(n_groups):
        u = c_ref[g].astype(F32) * h_ref[g].astype(F32)
        up = cp_ref[g, halo - 1:halo, :].astype(F32) * hp_ref[g, halo - 1:halo, :].astype(F32)
        un = cn_ref[g, 0:1, :].astype(F32) * hn_ref[g, 0:1, :].astype(F32)
        up = jnp.where(first, 0.0, up)
        un = jnp.where(last, 0.0, un)
        u_prev = jnp.where(row == 0, up, pltpu.roll(u, 1, axis=0))
        u_next = jnp.where(row == tm - 1, un, pltpu.roll(u, tm - 1, axis=0))
        conv = (cw_ref[0, g:g + 1, :] * u_prev + cw_ref[1, g:g + 1, :] * u
                + cw_ref[2, g:g + 1, :] * u_next)
        y = bg_ref[g].astype(F32) * conv
        y_ref[g] = y
        ss = ss + y * y
    ms = jnp.sum(ss, axis=-1, keepdims=True) * (1.0 / (n_groups * LANES))
    inv = lax.rsqrt(ms + EPS)
    for g in range(n_groups):
        z = z_ref[g].astype(F32)
        o_ref[:, g * LANES:(g + 1) * LANES] = (
            y_ref[g] * inv * nw_ref[g:g + 1, :] * (z * _sigmoid(z))).astype(o_ref.dtype)


def _conv(proj, cw, nw, seq, n_groups, first_slot, tm=256, halo=16):
    t = proj.shape[1]
    blocks_per_seq = seq // tm
    hb = tm // halo
    n_halo = t // halo

    def spec(slot):
        return pl.BlockSpec((n_groups, tm, LANES), lambda i, slot=slot: (slot, i, 0))

    def prev_spec(slot):
        return pl.BlockSpec((n_groups, halo, LANES),
                            lambda i, slot=slot: (slot, jnp.maximum(i * hb - 1, 0), 0))

    def next_spec(slot):
        return pl.BlockSpec((n_groups, halo, LANES),
                            lambda i, slot=slot: (slot, jnp.minimum((i + 1) * hb, n_halo - 1), 0))

    s_b, s_c, s_h, s_z = (first_slot + k for k in range(4))
    return pl.pallas_call(
        functools.partial(_conv_kernel, blocks_per_seq=blocks_per_seq),
        grid=(t // tm,),
        in_specs=[spec(s_b), spec(s_c), spec(s_h), spec(s_z),
                  prev_spec(s_c), prev_spec(s_h), next_spec(s_c), next_spec(s_h),
                  pl.BlockSpec(cw.shape, lambda i: (0, 0, 0)),
                  pl.BlockSpec(nw.shape, lambda i: (0, 0))],
        out_specs=pl.BlockSpec((tm, n_groups * LANES), lambda i: (i, 0)),
        out_shape=jax.ShapeDtypeStruct((t, n_groups * LANES), BF16),
        scratch_shapes=[pltpu.VMEM((n_groups, tm, LANES), F32)],
        compiler_params=pltpu.CompilerParams(
            dimension_semantics=("arbitrary",),
            vmem_limit_bytes=VMEM_LIMIT_BYTES),
        name="shortconv",
    )(proj, proj, proj, proj, proj, proj, proj, proj, cw, nw)


def _out_kernel(oa_ref, ob_ref, x_ref, p_ref, wo_ref, wg_ref, wp_ref, fw_ref, o_ref):
    d_a = oa_ref.shape[1]
    h1 = (x_ref[...]
          + jnp.dot(oa_ref[...], wo_ref[0:d_a, :], preferred_element_type=F32)
          + jnp.dot(ob_ref[...], wo_ref[d_a:, :], preferred_element_type=F32))
    gate = _sigmoid(jnp.dot(h1.astype(BF16), wg_ref[...], preferred_element_type=F32))
    ple = jnp.dot(p_ref[...].astype(BF16), wp_ref[...], preferred_element_type=F32)
    h2 = h1 + ple * gate
    ms = jnp.mean(h2 * h2, axis=-1, keepdims=True)
    o_ref[...] = h2 * lax.rsqrt(ms + EPS) * fw_ref[...]


def _out(oa, ob, x2d, p2d, wo, wg, wp, fw, tm=256):
    t, d = x2d.shape

    def resident(shape):
        return pl.BlockSpec(shape, lambda i: (0, 0), pipeline_mode=pl.Buffered(1))

    return pl.pallas_call(
        _out_kernel,
        grid=(t // tm,),
        in_specs=[pl.BlockSpec((tm, oa.shape[1]), lambda i: (i, 0)),
                  pl.BlockSpec((tm, ob.shape[1]), lambda i: (i, 0)),
                  pl.BlockSpec((tm, d), lambda i: (i, 0)),
                  pl.BlockSpec((tm, p2d.shape[1]), lambda i: (i, 0)),
                  resident(wo.shape), resident(wg.shape), resident(wp.shape),
                  resident((1, d))],
        out_specs=pl.BlockSpec((tm, d), lambda i: (i, 0)),
        out_shape=jax.ShapeDtypeStruct((t, d), F32),
        compiler_params=pltpu.CompilerParams(
            dimension_semantics=("arbitrary",),
            vmem_limit_bytes=VMEM_LIMIT_BYTES),
        name="outproj",
    )(oa, ob, x2d, p2d, wo, wg, wp, fw.reshape(1, d))


def kernel(x, p, norm_w, w_in, lb_theta, hgrn_norm_w, conv_w, conv_norm_w, w_out, w_ple,
           w_ple_gate, final_norm_w):
    batch, seq, d = x.shape
    depth = p.shape[0]
    assert depth == 1 and lb_theta.shape[1] == 2
    t = batch * seq
    n_heads = d // LANES
    n_groups = d // LANES

    x2d = x.reshape(t, d)
    hn = _rmsnorm(x2d, norm_w[0])
    proj = _inproj(hn, w_in[0])

    theta = lb_theta.reshape(2, 2, n_heads, LANES)
    o_a = _hgrn(proj, theta, hgrn_norm_w[0].reshape(n_heads, LANES), batch, seq, n_heads)
    o_b = _conv(proj, conv_w[0].reshape(3, n_groups, LANES),
                conv_norm_w[0].reshape(n_groups, LANES), seq, n_groups, first_slot=5)

    out = _out(o_a, o_b, x2d, p[0].reshape(t, -1),
               w_out[0].astype(BF16), w_ple_gate[0].astype(BF16), w_ple[0].astype(BF16),
               final_norm_w)
    return out.reshape(batch, seq, d)
```

```python
import functools

import jax
import jax.numpy as jnp
from jax import lax
from jax.experimental import pallas as pl
from jax.experimental.pallas import tpu as pltpu

EPS = 1e-6
LANES = 128
CHUNK = 64
GROUP = 8
HEADS_PER_ITEM = 2
N_HGRN_SLICES = 5
FINISH_GROUP = 16
PROJ_ROWS_PER_FINISH_ITER = 128
VMEM_CAPACITY_BYTES = 64 * 1024 * 1024
VMEM_LIMIT_BYTES = VMEM_CAPACITY_BYTES - 4 * 1024 * 1024

F32 = jnp.float32
BF16 = jnp.bfloat16


def _sigmoid(x):
    return 1.0 / (1.0 + jnp.exp(-x))


def _dot(a, b, dims=((1,), (0,))):
    return lax.dot_general(a, b, (dims, ((), ())), preferred_element_type=F32)


def _rmsnorm_kernel(x_ref, w_ref, o_ref):
    x = x_ref[...]
    ms = jnp.mean(x * x, axis=-1, keepdims=True)
    o_ref[...] = (x * lax.rsqrt(ms + EPS) * w_ref[...]).astype(o_ref.dtype)


def _rmsnorm(x2d, w, tm=512):
    t, d = x2d.shape
    return pl.pallas_call(
        _rmsnorm_kernel,
        grid=(t // tm,),
        in_specs=[pl.BlockSpec((tm, d), lambda i: (i, 0)),
                  pl.BlockSpec((1, d), lambda i: (0, 0))],
        out_specs=pl.BlockSpec((tm, d), lambda i: (i, 0)),
        out_shape=jax.ShapeDtypeStruct((t, d), BF16),
        name="rmsnorm",
    )(x2d, w.reshape(1, d))


def _hgrn_fused_kernel(hn_ref, wq_ref, wv_ref, wff_ref, wfb_ref, wz_ref, theta_ref, nw_ref,
                       o_ref, wbf_ref, proj0_ref, proj1_ref, *scratch, items_per_batch):
    s = pl.program_id(0)
    wcols = HEADS_PER_ITEM * LANES

    @pl.when(s == 0)
    def _():
        proj1_ref[...] = jnp.zeros(proj1_ref.shape, proj1_ref.dtype)

    for k, w_ref in enumerate((wq_ref, wv_ref, wff_ref, wfb_ref, wz_ref)):
        wbf_ref[:, k * wcols:(k + 1) * wcols] = w_ref[...].astype(BF16)

    args = (hn_ref, theta_ref, nw_ref, o_ref, wbf_ref)

    @pl.when(s % 2 == 0)
    def _():
        _hgrn_step(*args, proj0_ref, proj1_ref, *scratch, items_per_batch=items_per_batch)

    @pl.when(s % 2 == 1)
    def _():
        _hgrn_step(*args, proj1_ref, proj0_ref, *scratch, items_per_batch=items_per_batch)


def _hgrn_step(hn_ref, theta_ref, nw_ref, o_ref, wbf_ref, proj_w_ref, proj_r_ref,
               acc_ref, qin_ref, kv_ref, dec_ref, sprev_ref, *, items_per_batch):
    seq = hn_ref.shape[0]
    c = CHUNK
    n_chunks = seq // c
    gsz = GROUP * c
    n_groups = n_chunks // GROUP
    wcols = HEADS_PER_ITEM * LANES
    fgsz = FINISH_GROUP * c
    n_fgroups = n_chunks // FINISH_GROUP
    n_iters = HEADS_PER_ITEM * n_groups
    n_fiters = HEADS_PER_ITEM * n_fgroups
    finish_rows = PROJ_ROWS_PER_FINISH_ITER
    intra_rows = (seq - n_fiters * finish_rows) // n_iters
    assert n_iters * intra_rows + n_fiters * finish_rows == seq
    assert intra_rows % 16 == 0 and finish_rows % 16 == 0
    mid = c // 2 - 1

    item_h = jnp.maximum(pl.program_id(0) - 1, 0)
    head0 = (item_h % items_per_batch) * HEADS_PER_ITEM

    def project(start, size, slices):
        r = pl.ds(pl.multiple_of(start, 16), size)
        lhs = hn_ref[r, :]
        for k in slices:
            cols = slice(k * wcols, (k + 1) * wcols)
            proj_w_ref[r, cols] = jnp.dot(
                lhs, wbf_ref[:, cols], preferred_element_type=F32).astype(proj_w_ref.dtype)

    row = lax.broadcasted_iota(jnp.int32, (c, c), 0)
    col = lax.broadcasted_iota(jnp.int32, (c, c), 1)
    masks = (col <= row, col >= row)
    row2 = lax.broadcasted_iota(jnp.int32, (c, 2 * c), 0)
    col2 = lax.broadcasted_iota(jnp.int32, (c, 2 * c), 1) % c
    tris = ((col2 <= row2).astype(BF16), (col2 >= row2).astype(BF16))
    mid_rows = (mid, c - 1 - mid)
    last_rows = (c - 1, 0)

    def rows(a, j):
        return a[j * c:(j + 1) * c]

    for hh in range(HEADS_PER_ITEM):
        head = head0 + hh
        iter0 = hh * n_groups

        def load(k, r, hh=hh):
            lo = k * wcols + hh * LANES
            return proj_r_ref[r, lo:lo + LANES]

        lbs = []
        for d in range(2):
            t0 = theta_ref[d, 0, pl.ds(head, 1), :]
            t1 = theta_ref[d, 1, pl.ds(head, 1), :]
            m = jnp.maximum(t0, t1)
            e0 = jnp.exp(t0 - m)
            e1 = jnp.exp(t1 - m)
            lbs.append(e0 / (e0 + e1))

        def intra(g, carry, load=load, lbs=lbs, iter0=iter0):
            r = pl.ds(pl.multiple_of(g * gsz, gsz), gsz)
            prow = (iter0 + g) * intra_rows
            project(prow, intra_rows, (0,))
            q = load(0, r).astype(F32)
            v = load(1, r)
            qs = q * _sigmoid(q)
            ks, bs = [], []
            for d in range(2):
                x = load(2 + d, r).astype(F32)
                f = lbs[d] + (1.0 - lbs[d]) * _sigmoid(x)
                lf = jnp.log(f)
                ks.append(1.0 - f)
                hi = lf.astype(BF16)
                lo = (lf - hi.astype(F32)).astype(BF16)
                bs.append([_dot(tris[d], jnp.concatenate([rows(hi, j), rows(lo, j)], axis=0))
                           for j in range(GROUP)])
            project(prow, intra_rows, (1, 2))
            ss = ([], [])
            for j in range(GROUP):
                n = g * GROUP + j
                rj = pl.ds(pl.multiple_of(n * c, c), c)
                for d in range(2):
                    b = bs[d][j]
                    b_mid = b[mid_rows[d]:mid_rows[d] + 1, :]
                    b_last = b[last_rows[d]:last_rows[d] + 1, :]
                    q_rel = rows(qs, j) * jnp.exp(b - b_mid)
                    k_rel = rows(ks[d], j) * jnp.exp(b_mid - b)
                    ss[d].append(_dot(q_rel.astype(BF16), k_rel.astype(BF16), ((1,), (1,))))
                    q_in = q_rel * jnp.exp(b_mid)
                    k_st = k_rel * jnp.exp(b_last - b_mid)
                    qin_ref[d, rj, :] = q_in.astype(BF16)
                    kv_ref[d, n] = _dot(rows(v, j), k_st.astype(BF16), ((0,), (0,)))
                    dec_ref[d, n] = jnp.broadcast_to(jnp.exp(b_last), (8, LANES))
            project(prow, intra_rows, (3,))
            for j in range(GROUP):
                rj = pl.ds(pl.multiple_of((g * GROUP + j) * c, c), c)
                p = jnp.where(masks[0], ss[0][j], 0.0) + jnp.where(masks[1], ss[1][j], 0.0)
                acc_ref[rj, :] = _dot(p.astype(BF16), rows(v, j))
            project(prow, intra_rows, (4,))
            return carry

        lax.fori_loop(0, n_groups, intra, 0)

        def scan(i, sts):
            st_f, st_b = sts
            nb = n_chunks - 1 - i
            sprev_ref[i, :, 0:LANES] = st_f.astype(BF16)
            sprev_ref[nb, :, LANES:2 * LANES] = st_b.astype(BF16)
            return (st_f * dec_ref[0, i][0:1, :] + kv_ref[0, i],
                    st_b * dec_ref[1, nb][0:1, :] + kv_ref[1, nb])

        zero = jnp.zeros((LANES, LANES), F32)
        lax.fori_loop(0, n_chunks, scan, (zero, zero))

        nw = nw_ref[pl.ds(head, 1), :]

        def finish(g, carry, load=load, nw=nw, hh=hh, iter0=iter0):
            r = pl.ds(pl.multiple_of(g * fgsz, fgsz), fgsz)
            inter = []
            for j in range(FINISH_GROUP):
                n = g * FINISH_GROUP + j
                rj = pl.ds(pl.multiple_of(n * c, c), c)
                qin = jnp.concatenate([qin_ref[0, rj, :], qin_ref[1, rj, :]], axis=1)
                inter.append(_dot(qin, sprev_ref[n], ((1,), (1,))))
            project(n_iters * intra_rows + (hh * n_fgroups + g) * finish_rows, finish_rows,
                    range(N_HGRN_SLICES))
            o = acc_ref[r, :] + jnp.concatenate(inter, axis=0)
            ms = jnp.mean(o * o, axis=-1, keepdims=True)
            z = load(4, r).astype(F32)
            o_ref[r, hh * LANES:(hh + 1) * LANES] = (
                o * lax.rsqrt(ms + EPS) * nw * (z * _sigmoid(z))).astype(o_ref.dtype)
            return carry

        lax.fori_loop(0, n_fgroups, finish, 0)


def _hgrn_fused(hn, w, theta, nw, batch, seq, n_heads):
    t, d = hn.shape
    n_chunks = seq // CHUNK
    items_per_batch = n_heads // HEADS_PER_ITEM
    n_items = batch * items_per_batch
    wcols = HEADS_PER_ITEM * LANES
    slice_blocks = n_heads * LANES // wcols

    def item_m(s):
        return jnp.minimum(s, n_items - 1)

    def item_h(s):
        return jnp.maximum(s - 1, 0)

    def w_spec(k):
        return pl.BlockSpec(
            (d, wcols), lambda s, k=k: (0, k * slice_blocks + item_m(s) % items_per_batch))

    kern = functools.partial(_hgrn_fused_kernel, items_per_batch=items_per_batch)
    return pl.pallas_call(
        kern,
        grid=(n_items + 1,),
        in_specs=[pl.BlockSpec((seq, d), lambda s: (item_m(s) // items_per_batch, 0),
                               pipeline_mode=pl.Buffered(1)),
                  w_spec(0), w_spec(1), w_spec(2), w_spec(3), w_spec(4),
                  pl.BlockSpec(theta.shape, lambda s: (0, 0, 0, 0)),
                  pl.BlockSpec(nw.shape, lambda s: (0, 0))],
        out_specs=pl.BlockSpec(
            (seq, wcols), lambda s: (item_h(s) // items_per_batch, item_h(s) % items_per_batch)),
        out_shape=jax.ShapeDtypeStruct((t, n_heads * LANES), BF16),
        scratch_shapes=[
            pltpu.VMEM((d, N_HGRN_SLICES * wcols), BF16),
            pltpu.VMEM((seq, N_HGRN_SLICES * wcols), BF16),
            pltpu.VMEM((seq, N_HGRN_SLICES * wcols), BF16),
            pltpu.VMEM((seq, LANES), F32),
            pltpu.VMEM((2, seq, LANES), BF16),
            pltpu.VMEM((2, n_chunks, LANES, LANES), F32),
            pltpu.VMEM((2, n_chunks, 8, LANES), F32),
            pltpu.VMEM((n_chunks, LANES, 2 * LANES), BF16),
        ],
        compiler_params=pltpu.CompilerParams(
            dimension_semantics=("arbitrary",),
            vmem_limit_bytes=VMEM_LIMIT_BYTES),
        name="hgrn2_fused",
    )(hn, w, w, w, w, w, theta, nw)


def _inproj_kernel(h_ref, w_ref, o_ref, wbf_ref):
    @pl.when(pl.program_id(1) == 0)
    def _():
        wbf_ref[...] = w_ref[...].astype(BF16)

    acc = jnp.dot(h_ref[...], wbf_ref[...], preferred_element_type=F32)
    for g in range(o_ref.shape[0]):
        o_ref[g] = acc[:, g * LANES:(g + 1) * LANES].astype(o_ref.dtype)


def _inproj(hn, w, col_start, n_cols, tm=1024, tn=1024):
    t, d = hn.shape
    gpb = tn // LANES
    off = col_start // tn
    return pl.pallas_call(
        _inproj_kernel,
        grid=(n_cols // tn, t // tm),
        in_specs=[pl.BlockSpec((tm, d), lambda j, i: (i, 0)),
                  pl.BlockSpec((d, tn), lambda j, i: (0, j + off))],
        out_specs=pl.BlockSpec((gpb, tm, LANES), lambda j, i: (j, i, 0)),
        out_shape=jax.ShapeDtypeStruct((n_cols // LANES, t, LANES), BF16),
        scratch_shapes=[pltpu.VMEM((d, tn), BF16)],
        compiler_params=pltpu.CompilerParams(
            dimension_semantics=("arbitrary", "arbitrary"),
            vmem_limit_bytes=VMEM_LIMIT_BYTES),
        name="inproj",
    )(hn, w)


def _conv_kernel(bg_ref, c_ref, h_ref, z_ref, cp_ref, hp_ref, cn_ref, hn_ref,
                 cw_ref, nw_ref, o_ref, y_ref, *, blocks_per_seq):
    n_groups, tm, _ = c_ref.shape
    halo = cp_ref.shape[1]
    i = pl.program_id(0)
    first = (i % blocks_per_seq) == 0
    last = (i % blocks_per_seq) == blocks_per_seq - 1
    row = lax.broadcasted_iota(jnp.int32, (tm, LANES), 0)

    ss = jnp.zeros((tm, LANES), F32)
    for g in range(n_groups):
        u = c_ref[g].astype(F32) * h_ref[g].astype(F32)
        up = cp_ref[g, halo - 1:halo, :].astype(F32) * hp_ref[g, halo - 1:halo, :].astype(F32)
        un = cn_ref[g, 0:1, :].astype(F32) * hn_ref[g, 0:1, :].astype(F32)
        up = jnp.where(first, 0.0, up)
        un = jnp.where(last, 0.0, un)
        u_prev = jnp.where(row == 0, up, pltpu.roll(u, 1, axis=0))
        u_next = jnp.where(row == tm - 1, un, pltpu.roll(u, tm - 1, axis=0))
        conv = (cw_ref[0, g:g + 1, :] * u_prev + cw_ref[1, g:g + 1, :] * u
                + cw_ref[2, g:g + 1, :] * u_next)
        y = bg_ref[g].astype(F32) * conv
        y_ref[g] = y
        ss = ss + y * y
    ms = jnp.sum(ss, axis=-1, keepdims=True) * (1.0 / (n_groups * LANES))
    inv = lax.rsqrt(ms + EPS)
    for g in range(n_groups):
        z = z_ref[g].astype(F32)
        o_ref[:, g * LANES:(g + 1) * LANES] = (
            y_ref[g] * inv * nw_ref[g:g + 1, :] * (z * _sigmoid(z))).astype(o_ref.dtype)


def _conv(proj, cw, nw, seq, n_groups, first_slot, tm=256, halo=16):
    t = proj.shape[1]
    blocks_per_seq = seq // tm
    hb = tm // halo
    n_halo = t // halo

    def spec(slot):
        return pl.BlockSpec((n_groups, tm, LANES), lambda i, slot=slot: (slot, i, 0))

    def prev_spec(slot):
        return pl.BlockSpec((n_groups, halo, LANES),
                            lambda i, slot=slot: (slot, jnp.maximum(i * hb - 1, 0), 0))

    def next_spec(slot):
        return pl.BlockSpec((n_groups, halo, LANES),
                            lambda i, slot=slot: (slot, jnp.minimum((i + 1) * hb, n_halo - 1), 0))

    s_b, s_c, s_h, s_z = (first_slot + k for k in range(4))
    return pl.pallas_call(
        functools.partial(_conv_kernel, blocks_per_seq=blocks_per_seq),
        grid=(t // tm,),
        in_specs=[spec(s_b), spec(s_c), spec(s_h), spec(s_z),
                  prev_spec(s_c), prev_spec(s_h), next_spec(s_c), next_spec(s_h),
                  pl.BlockSpec(cw.shape, lambda i: (0, 0, 0)),
                  pl.BlockSpec(nw.shape, lambda i: (0, 0))],
        out_specs=pl.BlockSpec((tm, n_groups * LANES), lambda i: (i, 0)),
        out_shape=jax.ShapeDtypeStruct((t, n_groups * LANES), BF16),
        scratch_shapes=[pltpu.VMEM((n_groups, tm, LANES), F32)],
        compiler_params=pltpu.CompilerParams(
            dimension_semantics=("arbitrary",),
            vmem_limit_bytes=VMEM_LIMIT_BYTES),
        name="shortconv",
    )(proj, proj, proj, proj, proj, proj, proj, proj, cw, nw)


def _out_kernel(oa_ref, ob_ref, x_ref, p_ref, wo_ref, wg_ref, wp_ref, fw_ref, o_ref):
    d_a = oa_ref.shape[1]
    h1 = (x_ref[...]
          + jnp.dot(oa_ref[...], wo_ref[0:d_a, :], preferred_element_type=F32)
          + jnp.dot(ob_ref[...], wo_ref[d_a:, :], preferred_element_type=F32))
    gate = _sigmoid(jnp.dot(h1.astype(BF16), wg_ref[...], preferred_element_type=F32))
    ple = jnp.dot(p_ref[...].astype(BF16), wp_ref[...], preferred_element_type=F32)
    h2 = h1 + ple * gate
    ms = jnp.mean(h2 * h2, axis=-1, keepdims=True)
    o_ref[...] = h2 * lax.rsqrt(ms + EPS) * fw_ref[...]


def _out(oa, ob, x2d, p2d, wo, wg, wp, fw, tm=256):
    t, d = x2d.shape

    def resident(shape):
        return pl.BlockSpec(shape, lambda i: (0, 0), pipeline_mode=pl.Buffered(1))

    return pl.pallas_call(
        _out_kernel,
        grid=(t // tm,),
        in_specs=[pl.BlockSpec((tm, oa.shape[1]), lambda i: (i, 0)),
                  pl.BlockSpec((tm, ob.shape[1]), lambda i: (i, 0)),
                  pl.BlockSpec((tm, d), lambda i: (i, 0)),
                  pl.BlockSpec((tm, p2d.shape[1]), lambda i: (i, 0)),
                  resident(wo.shape), resident(wg.shape), resident(wp.shape),
                  resident((1, d))],
        out_specs=pl.BlockSpec((tm, d), lambda i: (i, 0)),
        out_shape=jax.ShapeDtypeStruct((t, d), F32),
        compiler_params=pltpu.CompilerParams(
            dimension_semantics=("arbitrary",),
            vmem_limit_bytes=VMEM_LIMIT_BYTES),
        name="outproj",
    )(oa, ob, x2d, p2d, wo, wg, wp, fw.reshape(1, d))


def kernel(x, p, norm_w, w_in, lb_theta, hgrn_norm_w, conv_w, conv_norm_w, w_out, w_ple,
           w_ple_gate, final_norm_w):
    batch, seq, d = x.shape
    depth = p.shape[0]
    assert depth == 1 and lb_theta.shape[1] == 2
    t = batch * seq
    n_heads = d // LANES
    n_groups = d // LANES

    x2d = x.reshape(t, d)
    hn = _rmsnorm(x2d, norm_w[0])
    w = w_in[0]

    theta = lb_theta.reshape(2, 2, n_heads, LANES)
    o_a = _hgrn_fused(hn, w, theta, hgrn_norm_w[0].reshape(n_heads, LANES), batch, seq, n_heads)

    proj_b = _inproj(hn, w, col_start=N_HGRN_SLICES * d, n_cols=4 * d)
    o_b = _conv(proj_b, conv_w[0].reshape(3, n_groups, LANES),
                conv_norm_w[0].reshape(n_groups, LANES), seq, n_groups, first_slot=0)

    out = _out(o_a, o_b, x2d, p[0].reshape(t, -1),
               w_out[0].astype(BF16), w_ple_gate[0].astype(BF16), w_ple[0].astype(BF16),
               final_norm_w)
    return out.reshape(batch, seq, d)
```

```python
import functools

import jax
import jax.numpy as jnp
from jax import lax
from jax.experimental import pallas as pl
from jax.experimental.pallas import tpu as pltpu

EPS = 1e-6
LANES = 128
CHUNK = 64
GROUP = 32
VMEM_CAPACITY_BYTES = 64 * 1024 * 1024
VMEM_LIMIT_BYTES = VMEM_CAPACITY_BYTES - 8 * 1024 * 1024

F32 = jnp.float32
BF16 = jnp.bfloat16


def _sigmoid(x):
    return 1.0 / (1.0 + jnp.exp(-x))


def _dot(a, b, dims=((1,), (0,))):
    return lax.dot_general(a, b, (dims, ((), ())), preferred_element_type=F32)


def _rmsnorm_kernel(x_ref, w_ref, o_ref):
    x = x_ref[...]
    ms = jnp.mean(x * x, axis=-1, keepdims=True)
    o_ref[...] = (x * lax.rsqrt(ms + EPS) * w_ref[...]).astype(o_ref.dtype)


def _rmsnorm(x2d, w, tm=512):
    t, d = x2d.shape
    return pl.pallas_call(
        _rmsnorm_kernel,
        grid=(t // tm,),
        in_specs=[pl.BlockSpec((tm, d), lambda i: (i, 0)),
                  pl.BlockSpec((1, d), lambda i: (0, 0))],
        out_specs=pl.BlockSpec((tm, d), lambda i: (i, 0)),
        out_shape=jax.ShapeDtypeStruct((t, d), BF16),
        name="rmsnorm",
    )(x2d, w.reshape(1, d))


def _store_heads(o_ref, val):
    for g in range(o_ref.shape[0]):
        o_ref[g] = val[:, g * LANES:(g + 1) * LANES].astype(o_ref.dtype)


def _inproj_kernel(h_ref, w_ref, *refs, epilogue):
    wbf_ref = refs[-1]

    @pl.when(pl.program_id(1) == 0)
    def _():
        wbf_ref[...] = w_ref[...].astype(BF16)

    acc = jnp.dot(h_ref[...], wbf_ref[...], preferred_element_type=F32)
    if epilogue == "plain":
        _store_heads(refs[0], acc)
    elif epilogue == "silu":
        _store_heads(refs[0], acc * _sigmoid(acc))
    else:
        theta_ref, hi_ref, lo_ref, k_ref = refs[:4]
        t0 = theta_ref[0:1, :]
        t1 = theta_ref[1:2, :]
        m = jnp.maximum(t0, t1)
        e0 = jnp.exp(t0 - m)
        e1 = jnp.exp(t1 - m)
        lb = e0 / (e0 + e1)
        f = lb + (1.0 - lb) * _sigmoid(acc)
        lf = jnp.log(f)
        hi = lf.astype(BF16)
        _store_heads(hi_ref, hi)
        _store_heads(lo_ref, lf - hi.astype(F32))
        _store_heads(k_ref, 1.0 - f)


def _inproj(hn, w, col_start, n_cols, epilogue, theta=None, tm=1024, tn=1024):
    t, d = hn.shape
    gpb = tn // LANES
    off = col_start // tn
    out_spec = pl.BlockSpec((gpb, tm, LANES), lambda j, i: (j, i, 0))
    out_sds = jax.ShapeDtypeStruct((n_cols // LANES, t, LANES), BF16)
    in_specs = [pl.BlockSpec((tm, d), lambda j, i: (i, 0)),
                pl.BlockSpec((d, tn), lambda j, i: (0, j + off))]
    args = [hn, w]
    n_out = 1
    if epilogue == "gate":
        in_specs.append(pl.BlockSpec((theta.shape[0], tn), lambda j, i: (0, j)))
        args.append(theta)
        n_out = 3
    res = pl.pallas_call(
        functools.partial(_inproj_kernel, epilogue=epilogue),
        grid=(n_cols // tn, t // tm),
        in_specs=in_specs,
        out_specs=[out_spec] * n_out,
        out_shape=[out_sds] * n_out,
        scratch_shapes=[pltpu.VMEM((d, tn), BF16)],
        compiler_params=pltpu.CompilerParams(
            dimension_semantics=("arbitrary", "arbitrary"),
            vmem_limit_bytes=VMEM_LIMIT_BYTES),
        name="inproj_" + epilogue,
    )(*args)
    return res if n_out > 1 else res[0]


def _convproj_kernel(h_ref, wb_ref, wc_ref, wh_ref, wz_ref, cw_ref, y_ref, zs_ref, wbf_ref):
    seq = h_ref.shape[0]
    tnc = wb_ref.shape[1]

    @pl.when(pl.program_id(1) == 0)
    def _():
        for k, w_ref in enumerate((wb_ref, wc_ref, wh_ref, wz_ref)):
            wbf_ref[k] = w_ref[...].astype(BF16)

    h = h_ref[...]
    bg, cg, hb, zb = (jnp.dot(h, wbf_ref[k], preferred_element_type=F32) for k in range(4))
    u = cg * hb
    row = lax.broadcasted_iota(jnp.int32, u.shape, 0)
    u_prev = jnp.where(row == 0, 0.0, pltpu.roll(u, 1, axis=0))
    u_next = jnp.where(row == seq - 1, 0.0, pltpu.roll(u, seq - 1, axis=0))
    conv = cw_ref[0:1, :] * u_prev + cw_ref[1:2, :] * u + cw_ref[2:3, :] * u_next
    y_ref[...] = (bg * conv).astype(y_ref.dtype)
    zs_ref[...] = (zb * _sigmoid(zb)).astype(zs_ref.dtype)


def _convproj(hn, w, cw, col_start, seq, n_ch, tnc=256):
    t, d = hn.shape
    batch = t // seq
    blk0 = col_start // tnc
    per_slice = n_ch // tnc

    def w_spec(k):
        return pl.BlockSpec((d, tnc), lambda c, b, k=k: (0, blk0 + k * per_slice + c))

    out_spec = pl.BlockSpec((seq, tnc), lambda c, b: (b, c))
    out_sds = jax.ShapeDtypeStruct((t, n_ch), BF16)
    return pl.pallas_call(
        _convproj_kernel,
        grid=(per_slice, batch),
        in_specs=[pl.BlockSpec((seq, d), lambda c, b: (b, 0)),
                  w_spec(0), w_spec(1), w_spec(2), w_spec(3),
                  pl.BlockSpec((cw.shape[0], tnc), lambda c, b: (0, c))],
        out_specs=[out_spec, out_spec],
        out_shape=[out_sds, out_sds],
        scratch_shapes=[pltpu.VMEM((4, d, tnc), BF16)],
        compiler_params=pltpu.CompilerParams(
            dimension_semantics=("arbitrary", "arbitrary"),
            vmem_limit_bytes=VMEM_LIMIT_BYTES),
        name="convproj",
    )(hn, w, w, w, w, cw)


def _hgrn_kernel(qs_ref, v_ref, hif_ref, lof_ref, kf_ref, hib_ref, lob_ref, kb_ref, zs_ref,
                 nw_ref, o_ref, acc_ref, qin_ref, kv_ref, dec_ref, sprev_ref):
    seq = qs_ref.shape[1]
    n_chunks = seq // CHUNK
    head = pl.program_id(1)
    c = CHUNK
    mid = c // 2 - 1

    row = lax.broadcasted_iota(jnp.int32, (c, c), 0)
    col = lax.broadcasted_iota(jnp.int32, (c, c), 1)
    masks = (col <= row, col >= row)
    row2 = lax.broadcasted_iota(jnp.int32, (c, 2 * c), 0)
    col2 = lax.broadcasted_iota(jnp.int32, (c, 2 * c), 1) % c
    tris = ((col2 <= row2).astype(BF16), (col2 >= row2).astype(BF16))
    hi_refs = (hif_ref, hib_ref)
    lo_refs = (lof_ref, lob_ref)
    k_refs = (kf_ref, kb_ref)
    mid_rows = (mid, c - 1 - mid)
    last_rows = (c - 1, 0)

    gsz = GROUP * c
    n_groups = n_chunks // GROUP

    def rows(a, j):
        return a[j * c:(j + 1) * c]

    def intra(g, carry):
        r = pl.ds(pl.multiple_of(g * gsz, gsz), gsz)
        qs = qs_ref[0, r, :].astype(F32)
        v = v_ref[0, r, :]
        ks, bs = [], []
        for d in range(2):
            hi = hi_refs[d][0, r, :]
            lo = lo_refs[d][0, r, :]
            ks.append(k_refs[d][0, r, :].astype(F32))
            bs.append([_dot(tris[d], jnp.concatenate([rows(hi, j), rows(lo, j)], axis=0))
                       for j in range(GROUP)])
        ss = ([], [])
        for j in range(GROUP):
            n = g * GROUP + j
            rj = pl.ds(pl.multiple_of(n * c, c), c)
            for d in range(2):
                b = bs[d][j]
                b_mid = b[mid_rows[d]:mid_rows[d] + 1, :]
                b_last = b[last_rows[d]:last_rows[d] + 1, :]
                q_rel = rows(qs, j) * jnp.exp(b - b_mid)
                k_rel = rows(ks[d], j) * jnp.exp(b_mid - b)
                ss[d].append(_dot(q_rel.astype(BF16), k_rel.astype(BF16), ((1,), (1,))))
                q_in = q_rel * jnp.exp(b_mid)
                k_st = k_rel * jnp.exp(b_last - b_mid)
                qin_ref[d, rj, :] = q_in.astype(BF16)
                kv_ref[d, n] = _dot(rows(v, j), k_st.astype(BF16), ((0,), (0,)))
                dec_ref[d, n] = jnp.broadcast_to(jnp.exp(b_last), (8, LANES))
        for j in range(GROUP):
            rj = pl.ds(pl.multiple_of((g * GROUP + j) * c, c), c)
            p = jnp.where(masks[0], ss[0][j], 0.0) + jnp.where(masks[1], ss[1][j], 0.0)
            acc_ref[rj, :] = _dot(p.astype(BF16), rows(v, j))
        return carry

    lax.fori_loop(0, n_groups, intra, 0)

    def scan(i, sts):
        st_f, st_b = sts
        nb = n_chunks - 1 - i
        sprev_ref[i, :, 0:LANES] = st_f.astype(BF16)
        sprev_ref[nb, :, LANES:2 * LANES] = st_b.astype(BF16)
        return (st_f * dec_ref[0, i][0:1, :] + kv_ref[0, i],
                st_b * dec_ref[1, nb][0:1, :] + kv_ref[1, nb])

    zero = jnp.zeros((LANES, LANES), F32)
    lax.fori_loop(0, n_chunks, scan, (zero, zero))

    nw = nw_ref[pl.ds(head, 1), :]

    def finish(g, carry):
        r = pl.ds(pl.multiple_of(g * gsz, gsz), gsz)
        inter = []
        for j in range(GROUP):
            n = g * GROUP + j
            rj = pl.ds(pl.multiple_of(n * c, c), c)
            qin = jnp.concatenate([qin_ref[0, rj, :], qin_ref[1, rj, :]], axis=1)
            inter.append(_dot(qin, sprev_ref[n], ((1,), (1,))))
        o = acc_ref[r, :] + jnp.concatenate(inter, axis=0)
        ms = jnp.mean(o * o, axis=-1, keepdims=True)
        o_ref[r, :] = (o * lax.rsqrt(ms + EPS) * nw
                       * zs_ref[0, r, :].astype(F32)).astype(o_ref.dtype)
        return carry

    lax.fori_loop(0, n_groups, finish, 0)


def _hgrn(qs, v, hif, lof, kf, hib, lob, kb, zs, nw, batch, seq, n_heads):
    t = batch * seq
    n_chunks = seq // CHUNK
    spec = pl.BlockSpec((1, seq, LANES), lambda b, h: (h, b, 0))
    return pl.pallas_call(
        _hgrn_kernel,
        grid=(batch, n_heads),
        in_specs=[spec] * 9 + [pl.BlockSpec(nw.shape, lambda b, h: (0, 0))],
        out_specs=pl.BlockSpec((seq, LANES), lambda b, h: (b, h)),
        out_shape=jax.ShapeDtypeStruct((t, n_heads * LANES), BF16),
        scratch_shapes=[
            pltpu.VMEM((seq, LANES), F32),
            pltpu.VMEM((2, seq, LANES), BF16),
            pltpu.VMEM((2, n_chunks, LANES, LANES), F32),
            pltpu.VMEM((2, n_chunks, 8, LANES), F32),
            pltpu.VMEM((n_chunks, LANES, 2 * LANES), BF16),
        ],
        compiler_params=pltpu.CompilerParams(
            dimension_semantics=("arbitrary", "arbitrary"),
            vmem_limit_bytes=VMEM_LIMIT_BYTES),
        name="hgrn2",
    )(qs, v, hif, lof, kf, hib, lob, kb, zs, nw)


def _out_kernel(oa_ref, y_ref, zs_ref, x_ref, p_ref, wo_ref, wg_ref, wp_ref, cnw_ref, fw_ref,
                o_ref):
    d_a = oa_ref.shape[1]
    y = y_ref[...].astype(F32)
    ms_b = jnp.mean(y * y, axis=-1, keepdims=True)
    ob = (y * zs_ref[...].astype(F32) * (lax.rsqrt(ms_b + EPS) * cnw_ref[...])).astype(BF16)
    h1 = (x_ref[...]
          + jnp.dot(oa_ref[...], wo_ref[0:d_a, :], preferred_element_type=F32)
          + jnp.dot(ob, wo_ref[d_a:, :], preferred_element_type=F32))
    gate = _sigmoid(jnp.dot(h1.astype(BF16), wg_ref[...], preferred_element_type=F32))
    ple = jnp.dot(p_ref[...].astype(BF16), wp_ref[...], preferred_element_type=F32)
    h2 = h1 + ple * gate
    ms = jnp.mean(h2 * h2, axis=-1, keepdims=True)
    o_ref[...] = h2 * lax.rsqrt(ms + EPS) * fw_ref[...]


def _out(oa, y, zs, x2d, p2d, wo, wg, wp, cnw, fw, tm=256):
    t, d = x2d.shape

    def resident(shape):
        return pl.BlockSpec(shape, lambda i: (0, 0), pipeline_mode=pl.Buffered(1))

    def rows(a):
        return pl.BlockSpec((tm, a.shape[1]), lambda i: (i, 0))

    return pl.pallas_call(
        _out_kernel,
        grid=(t // tm,),
        in_specs=[rows(oa), rows(y), rows(zs), rows(x2d), rows(p2d),
                  resident(wo.shape), resident(wg.shape), resident(wp.shape),
                  resident((1, d)), resident((1, d))],
        out_specs=pl.BlockSpec((tm, d), lambda i: (i, 0)),
        out_shape=jax.ShapeDtypeStruct((t, d), F32),
        compiler_params=pltpu.CompilerParams(
            dimension_semantics=("arbitrary",),
            vmem_limit_bytes=VMEM_LIMIT_BYTES),
        name="outproj",
    )(oa, y, zs, x2d, p2d, wo, wg, wp, cnw.reshape(1, d), fw.reshape(1, d))


def kernel(x, p, norm_w, w_in, lb_theta, hgrn_norm_w, conv_w, conv_norm_w, w_out, w_ple,
           w_ple_gate, final_norm_w):
    batch, seq, d = x.shape
    depth = p.shape[0]
    assert depth == 1 and lb_theta.shape[1] == 2
    t = batch * seq
    n_heads = d // LANES

    x2d = x.reshape(t, d)
    hn = _rmsnorm(x2d, norm_w[0])
    w = w_in[0]

    qs = _inproj(hn, w, 0 * d, d, "silu")
    v = _inproj(hn, w, 1 * d, d, "plain")
    hif, lof, kf = _inproj(hn, w, 2 * d, d, "gate", theta=lb_theta[0])
    hib, lob, kb = _inproj(hn, w, 3 * d, d, "gate", theta=lb_theta[1])
    zs = _inproj(hn, w, 4 * d, d, "silu")
    o_a = _hgrn(qs, v, hif, lof, kf, hib, lob, kb, zs,
                hgrn_norm_w[0].reshape(n_heads, LANES), batch, seq, n_heads)

    y_b, zs_b = _convproj(hn, w, conv_w[0], 5 * d, seq, d)

    out = _out(o_a, y_b, zs_b, x2d, p[0].reshape(t, -1),
               w_out[0].astype(BF16), w_ple_gate[0].astype(BF16), w_ple[0].astype(BF16),
               conv_norm_w[0], final_norm_w)
    return out.reshape(batch, seq, d)
```

```python
import jax
import jax.numpy as jnp
from jax import lax
from jax.experimental import pallas as pl
from jax.experimental.pallas import tpu as pltpu

EPS = 1e-6
LANES = 128
CHUNK = 64
GROUP = 32
VMEM_CAPACITY_BYTES = 64 * 1024 * 1024
VMEM_LIMIT_BYTES = VMEM_CAPACITY_BYTES - 8 * 1024 * 1024

F32 = jnp.float32
BF16 = jnp.bfloat16


def _sigmoid(x):
    return 0.5 * jnp.tanh(0.5 * x) + 0.5


def _dot(a, b, dims=((1,), (0,))):
    return lax.dot_general(a, b, (dims, ((), ())), preferred_element_type=F32)


def _rmsnorm_kernel(x_ref, w_ref, o_ref):
    x = x_ref[...]
    ms = jnp.mean(x * x, axis=-1, keepdims=True)
    o_ref[...] = (x * lax.rsqrt(ms + EPS) * w_ref[...]).astype(o_ref.dtype)


def _rmsnorm(x2d, w, tm=512):
    t, d = x2d.shape
    return pl.pallas_call(
        _rmsnorm_kernel,
        grid=(t // tm,),
        in_specs=[pl.BlockSpec((tm, d), lambda i: (i, 0)),
                  pl.BlockSpec((1, d), lambda i: (0, 0))],
        out_specs=pl.BlockSpec((tm, d), lambda i: (i, 0)),
        out_shape=jax.ShapeDtypeStruct((t, d), BF16),
        name="rmsnorm",
    )(x2d, w.reshape(1, d))


def _store_heads(o_ref, val):
    for g in range(val.shape[1] // LANES):
        o_ref[g] = val[:, g * LANES:(g + 1) * LANES].astype(o_ref.dtype)


def _inproj_plain_kernel(h_ref, w_ref, o_ref, wbf_ref):
    @pl.when(pl.program_id(1) == 0)
    def _():
        wbf_ref[...] = w_ref[...].astype(BF16)

    _store_heads(o_ref, jnp.dot(h_ref[...], wbf_ref[...], preferred_element_type=F32))


def _inproj_plain(hn, w, col_start, n_cols, tm=1024, tn=1024):
    t, d = hn.shape
    off = col_start // tn
    return pl.pallas_call(
        _inproj_plain_kernel,
        grid=(n_cols // tn, t // tm),
        in_specs=[pl.BlockSpec((tm, d), lambda j, i: (i, 0)),
                  pl.BlockSpec((d, tn), lambda j, i: (0, j + off))],
        out_specs=pl.BlockSpec((tn // LANES, tm, LANES), lambda j, i: (j, i, 0)),
        out_shape=jax.ShapeDtypeStruct((n_cols // LANES, t, LANES), BF16),
        scratch_shapes=[pltpu.VMEM((d, tn), BF16)],
        compiler_params=pltpu.CompilerParams(
            dimension_semantics=("arbitrary", "arbitrary"),
            vmem_limit_bytes=VMEM_LIMIT_BYTES),
        name="inproj_plain",
    )(hn, w)


def _inproj_gate_silu_kernel(h_ref, wg_ref, ws_ref, theta_ref, hi_ref, lo_ref, k_ref, s_ref,
                             wbf_ref, acc_ref):
    tg = wg_ref.shape[1]

    @pl.when(pl.program_id(1) == 0)
    def _():
        wbf_ref[:, 0:tg] = wg_ref[...].astype(BF16)
        wbf_ref[:, tg:] = ws_ref[...].astype(BF16)

    h = h_ref[...]
    acc_ref[...] = jnp.dot(h, wbf_ref[:, 0:tg], preferred_element_type=F32)
    t0 = theta_ref[0:1, :]
    t1 = theta_ref[1:2, :]
    m = jnp.maximum(t0, t1)
    e0 = jnp.exp(t0 - m)
    e1 = jnp.exp(t1 - m)
    lb = e0 / (e0 + e1)
    f = 0.5 * (1.0 + lb) + (0.5 * (1.0 - lb)) * jnp.tanh(0.5 * acc_ref[...])
    lf = jnp.log(f)
    hi = lf.astype(BF16)
    _store_heads(hi_ref, hi)
    _store_heads(lo_ref, lf - hi.astype(F32))
    _store_heads(k_ref, 1.0 - f)
    acc = jnp.dot(h, wbf_ref[:, tg:], preferred_element_type=F32)
    _store_heads(s_ref, acc * _sigmoid(acc))


def _inproj_gate_silu(hn, w, theta2, gate_col_start, silu_col_starts, n_cols, tm=1024, tg=512):
    t, d = hn.shape
    n_blocks = 2 * n_cols // tg
    half = n_blocks // 2
    g0 = gate_col_start // tg
    s0, s1 = (cs // tg for cs in silu_col_starts)
    out_spec = pl.BlockSpec((tg // LANES, tm, LANES), lambda c, i: (c, i, 0))
    out_sds = jax.ShapeDtypeStruct((2 * n_cols // LANES, t, LANES), BF16)
    return pl.pallas_call(
        _inproj_gate_silu_kernel,
        grid=(n_blocks, t // tm),
        in_specs=[pl.BlockSpec((tm, d), lambda c, i: (i, 0)),
                  pl.BlockSpec((d, tg), lambda c, i: (0, g0 + c)),
                  pl.BlockSpec((d, tg),
                               lambda c, i: (0, jnp.where(c < half, s0 + c, s1 + c - half))),
                  pl.BlockSpec((theta2.shape[0], tg), lambda c, i: (0, c))],
        out_specs=[out_spec] * 4,
        out_shape=[out_sds] * 4,
        scratch_shapes=[pltpu.VMEM((d, 2 * tg), BF16), pltpu.VMEM((tm, tg), F32)],
        compiler_params=pltpu.CompilerParams(
            dimension_semantics=("arbitrary", "arbitrary"),
            vmem_limit_bytes=VMEM_LIMIT_BYTES),
        name="inproj_gate_silu",
    )(hn, w, w, theta2)


def _convproj_kernel(h_ref, wb_ref, wc_ref, wh_ref, wz_ref, cw_ref, y_ref, zs_ref, wbf_ref):
    seq = h_ref.shape[0]

    @pl.when(pl.program_id(1) == 0)
    def _():
        for k, w_ref in enumerate((wb_ref, wc_ref, wh_ref, wz_ref)):
            wbf_ref[k] = w_ref[...].astype(BF16)

    h = h_ref[...]
    cg = jnp.dot(h, wbf_ref[1], preferred_element_type=F32)
    hb = jnp.dot(h, wbf_ref[2], preferred_element_type=F32)
    u = cg * hb
    row = lax.broadcasted_iota(jnp.int32, u.shape, 0)
    u_prev = jnp.where(row == 0, 0.0, pltpu.roll(u, 1, axis=0))
    u_next = jnp.where(row == seq - 1, 0.0, pltpu.roll(u, seq - 1, axis=0))
    conv = cw_ref[0:1, :] * u_prev + cw_ref[1:2, :] * u + cw_ref[2:3, :] * u_next
    zb = jnp.dot(h, wbf_ref[3], preferred_element_type=F32)
    zs_ref[...] = (zb * _sigmoid(zb)).astype(zs_ref.dtype)
    bg = jnp.dot(h, wbf_ref[0], preferred_element_type=F32)
    y_ref[...] = (bg * conv).astype(y_ref.dtype)


def _convproj(hn, w, cw, col_start, seq, n_ch, tnc=256):
    t, d = hn.shape
    batch = t // seq
    blk0 = col_start // tnc
    per_slice = n_ch // tnc

    def w_spec(k):
        return pl.BlockSpec((d, tnc), lambda c, b, k=k: (0, blk0 + k * per_slice + c))

    out_spec = pl.BlockSpec((seq, tnc), lambda c, b: (b, c))
    out_sds = jax.ShapeDtypeStruct((t, n_ch), BF16)
    return pl.pallas_call(
        _convproj_kernel,
        grid=(per_slice, batch),
        in_specs=[pl.BlockSpec((seq, d), lambda c, b: (b, 0)),
                  w_spec(0), w_spec(1), w_spec(2), w_spec(3),
                  pl.BlockSpec((cw.shape[0], tnc), lambda c, b: (0, c))],
        out_specs=[out_spec, out_spec],
        out_shape=[out_sds, out_sds],
        scratch_shapes=[pltpu.VMEM((4, d, tnc), BF16)],
        compiler_params=pltpu.CompilerParams(
            dimension_semantics=("arbitrary", "arbitrary"),
            vmem_limit_bytes=VMEM_LIMIT_BYTES),
        name="convproj",
    )(hn, w, w, w, w, cw)


def _hgrn_kernel(qs_ref, v_ref, hif_ref, lof_ref, kf_ref, hib_ref, lob_ref, kb_ref, zs_ref,
                 nw_ref, o_ref, acc_ref, qin_ref, kv_ref, dec_ref, sprev_ref):
    seq = qs_ref.shape[1]
    n_chunks = seq // CHUNK
    head = pl.program_id(1)
    c = CHUNK
    mid = c // 2 - 1

    row = lax.broadcasted_iota(jnp.int32, (c, c), 0)
    col = lax.broadcasted_iota(jnp.int32, (c, c), 1)
    masks = (col <= row, col >= row)
    row2 = lax.broadcasted_iota(jnp.int32, (c, 2 * c), 0)
    col2 = lax.broadcasted_iota(jnp.int32, (c, 2 * c), 1) % c
    tris = ((col2 <= row2).astype(BF16), (col2 >= row2).astype(BF16))
    hi_refs = (hif_ref, hib_ref)
    lo_refs = (lof_ref, lob_ref)
    k_refs = (kf_ref, kb_ref)
    mid_rows = (mid, c - 1 - mid)
    last_rows = (c - 1, 0)

    gsz = GROUP * c
    n_groups = n_chunks // GROUP

    def rows(a, j):
        return a[j * c:(j + 1) * c]

    def intra(g, carry):
        r = pl.ds(pl.multiple_of(g * gsz, gsz), gsz)
        qs = qs_ref[0, r, :].astype(F32)
        v = v_ref[0, r, :]
        ks, bs = [], []
        for d in range(2):
            hi = hi_refs[d][0, r, :]
            lo = lo_refs[d][0, r, :]
            ks.append(k_refs[d][0, r, :].astype(F32))
            bs.append([_dot(tris[d], jnp.concatenate([rows(hi, j), rows(lo, j)], axis=0))
                       for j in range(GROUP)])
        ss = ([], [])
        for j in range(GROUP):
            n = g * GROUP + j
            rj = pl.ds(pl.multiple_of(n * c, c), c)
            for d in range(2):
                b = bs[d][j]
                b_mid = b[mid_rows[d]:mid_rows[d] + 1, :]
                b_last = b[last_rows[d]:last_rows[d] + 1, :]
                q_rel = rows(qs, j) * jnp.exp(b - b_mid)
                k_rel = rows(ks[d], j) * jnp.exp(b_mid - b)
                ss[d].append(_dot(q_rel.astype(BF16), k_rel.astype(BF16), ((1,), (1,))))
                q_in = q_rel * jnp.exp(b_mid)
                k_st = k_rel * jnp.exp(b_last - b_mid)
                qin_ref[d, rj, :] = q_in.astype(BF16)
                kv_ref[d, n] = _dot(rows(v, j), k_st.astype(BF16), ((0,), (0,)))
                dec_ref[d, n] = jnp.broadcast_to(jnp.exp(b_last), (8, LANES))
        for j in range(GROUP):
            rj = pl.ds(pl.multiple_of((g * GROUP + j) * c, c), c)
            p = jnp.where(masks[0], ss[0][j], 0.0) + jnp.where(masks[1], ss[1][j], 0.0)
            acc_ref[rj, :] = _dot(p.astype(BF16), rows(v, j))
        return carry

    lax.fori_loop(0, n_groups, intra, 0)

    def scan(i, sts):
        st_f, st_b = sts
        nb = n_chunks - 1 - i
        sprev_ref[i, :, 0:LANES] = st_f.astype(BF16)
        sprev_ref[nb, :, LANES:2 * LANES] = st_b.astype(BF16)
        return (st_f * dec_ref[0, i][0:1, :] + kv_ref[0, i],
                st_b * dec_ref[1, nb][0:1, :] + kv_ref[1, nb])

    zero = jnp.zeros((LANES, LANES), F32)
    lax.fori_loop(0, n_chunks, scan, (zero, zero))

    nw = nw_ref[pl.ds(head, 1), :]

    def finish(g, carry):
        r = pl.ds(pl.multiple_of(g * gsz, gsz), gsz)
        inter = []
        for j in range(GROUP):
            n = g * GROUP + j
            rj = pl.ds(pl.multiple_of(n * c, c), c)
            qin = jnp.concatenate([qin_ref[0, rj, :], qin_ref[1, rj, :]], axis=1)
            inter.append(_dot(qin, sprev_ref[n], ((1,), (1,))))
        o = acc_ref[r, :] + jnp.concatenate(inter, axis=0)
        ms = jnp.mean(o * o, axis=-1, keepdims=True)
        o_ref[r, :] = (o * lax.rsqrt(ms + EPS) * nw
                       * zs_ref[0, r, :].astype(F32)).astype(o_ref.dtype)
        return carry

    lax.fori_loop(0, n_groups, finish, 0)


def _hgrn(sil, v, hi, lo, k, nw, batch, seq, n_heads):
    t = batch * seq
    n_chunks = seq // CHUNK

    def spec(first):
        return pl.BlockSpec((1, seq, LANES), lambda b, h: (first + h, b, 0))

    fwd, bwd = spec(0), spec(n_heads)
    return pl.pallas_call(
        _hgrn_kernel,
        grid=(batch, n_heads),
        in_specs=[fwd, fwd, fwd, fwd, fwd, bwd, bwd, bwd, bwd,
                  pl.BlockSpec(nw.shape, lambda b, h: (0, 0))],
        out_specs=pl.BlockSpec((seq, LANES), lambda b, h: (b, h)),
        out_shape=jax.ShapeDtypeStruct((t, n_heads * LANES), BF16),
        scratch_shapes=[
            pltpu.VMEM((seq, LANES), F32),
            pltpu.VMEM((2, seq, LANES), BF16),
            pltpu.VMEM((2, n_chunks, LANES, LANES), F32),
            pltpu.VMEM((2, n_chunks, 8, LANES), F32),
            pltpu.VMEM((n_chunks, LANES, 2 * LANES), BF16),
        ],
        compiler_params=pltpu.CompilerParams(
            dimension_semantics=("arbitrary", "arbitrary"),
            vmem_limit_bytes=VMEM_LIMIT_BYTES),
        name="hgrn2",
    )(sil, v, hi, lo, k, hi, lo, k, sil, nw)


def _out_kernel(oa_ref, y_ref, zs_ref, x_ref, p_ref, wo_ref, wg_ref, wp_ref, cnw_ref, fw_ref,
                o_ref):
    d_a = oa_ref.shape[1]
    y = y_ref[...].astype(F32)
    ms_b = jnp.mean(y * y, axis=-1, keepdims=True)
    ob = (y * zs_ref[...].astype(F32) * (lax.rsqrt(ms_b + EPS) * cnw_ref[...])).astype(BF16)
    ple = jnp.dot(p_ref[...].astype(BF16), wp_ref[...], preferred_element_type=F32)
    h1 = (x_ref[...]
          + jnp.dot(oa_ref[...], wo_ref[0:d_a, :], preferred_element_type=F32)
          + jnp.dot(ob, wo_ref[d_a:, :], preferred_element_type=F32))
    gate = _sigmoid(jnp.dot(h1.astype(BF16), wg_ref[...], preferred_element_type=F32))
    h2 = h1 + ple * gate
    ms = jnp.mean(h2 * h2, axis=-1, keepdims=True)
    o_ref[...] = h2 * lax.rsqrt(ms + EPS) * fw_ref[...]


def _out(oa, y, zs, x2d, p2d, wo, wg, wp, cnw, fw, tm=256):
    t, d = x2d.shape

    def resident(shape):
        return pl.BlockSpec(shape, lambda i: (0, 0), pipeline_mode=pl.Buffered(1))

    def rows(a):
        return pl.BlockSpec((tm, a.shape[1]), lambda i: (i, 0))

    return pl.pallas_call(
        _out_kernel,
        grid=(t // tm,),
        in_specs=[rows(oa), rows(y), rows(zs), rows(x2d), rows(p2d),
                  resident(wo.shape), resident(wg.shape), resident(wp.shape),
                  resident((1, d)), resident((1, d))],
        out_specs=pl.BlockSpec((tm, d), lambda i: (i, 0)),
        out_shape=jax.ShapeDtypeStruct((t, d), F32),
        compiler_params=pltpu.CompilerParams(
            dimension_semantics=("arbitrary",),
            vmem_limit_bytes=VMEM_LIMIT_BYTES),
        name="outproj",
    )(oa, y, zs, x2d, p2d, wo, wg, wp, cnw.reshape(1, d), fw.reshape(1, d))


def kernel(x, p, norm_w, w_in, lb_theta, hgrn_norm_w, conv_w, conv_norm_w, w_out, w_ple,
           w_ple_gate, final_norm_w):
    batch, seq, d = x.shape
    depth = p.shape[0]
    assert depth == 1 and lb_theta.shape[1] == 2
    t = batch * seq
    n_heads = d // LANES

    x2d = x.reshape(t, d)
    hn = _rmsnorm(x2d, norm_w[0])
    w = w_in[0]

    v = _inproj_plain(hn, w, 1 * d, d)
    theta2 = jnp.transpose(lb_theta, (1, 0, 2)).reshape(2, 2 * d)
    hi, lo, k, sil = _inproj_gate_silu(hn, w, theta2, 2 * d, (0 * d, 4 * d), d)
    o_a = _hgrn(sil, v, hi, lo, k, hgrn_norm_w[0].reshape(n_heads, LANES), batch, seq, n_heads)

    y_b, zs_b = _convproj(hn, w, conv_w[0], 5 * d, seq, d)

    out = _out(o_a, y_b, zs_b, x2d, p[0].reshape(t, -1),
               w_out[0].astype(BF16), w_ple_gate[0].astype(BF16), w_ple[0].astype(BF16),
               conv_norm_w[0], final_norm_w)
    return out.reshape(batch, seq, d)
```

```python
import jax
import jax.numpy as jnp
from jax import lax
from jax.experimental import pallas as pl
from jax.experimental.pallas import tpu as pltpu

EPS = 1e-6
LANES = 128
BF16_SUBLANES = 16
CHUNK = 64
GROUP = 32
VMEM_CAPACITY_BYTES = 64 * 1024 * 1024
VMEM_LIMIT_BYTES = VMEM_CAPACITY_BYTES - 4 * 1024 * 1024

F32 = jnp.float32
BF16 = jnp.bfloat16


def _sigmoid(x):
    return 0.5 * jnp.tanh(0.5 * x) + 0.5


def _dot(a, b, dims=((1,), (0,))):
    return lax.dot_general(a, b, (dims, ((), ())), preferred_element_type=F32)


def _rmsnorm_kernel(x_ref, w_ref, o_ref):
    x = x_ref[...]
    ms = jnp.mean(x * x, axis=-1, keepdims=True)
    o_ref[...] = (x * lax.rsqrt(ms + EPS) * w_ref[...]).astype(o_ref.dtype)


def _rmsnorm(x2d, w, tm=512):
    t, d = x2d.shape
    return pl.pallas_call(
        _rmsnorm_kernel,
        grid=(t // tm,),
        in_specs=[pl.BlockSpec((tm, d), lambda i: (i, 0)),
                  pl.BlockSpec((1, d), lambda i: (0, 0))],
        out_specs=pl.BlockSpec((tm, d), lambda i: (i, 0)),
        out_shape=jax.ShapeDtypeStruct((t, d), BF16),
        name="rmsnorm",
    )(x2d, w.reshape(1, d))


def _store_heads(o_ref, val):
    for g in range(val.shape[1] // LANES):
        o_ref[g] = val[:, g * LANES:(g + 1) * LANES].astype(o_ref.dtype)


def _inproj_hgrn_kernel(h_ref, wg_ref, ws_ref, wv_ref, theta_ref, hi_ref, lo_ref, k_ref, s_ref,
                        v_ref, wbf_ref, acc_ref):
    tg = wg_ref.shape[1]
    ts = ws_ref.shape[1]

    @pl.when(pl.program_id(1) == 0)
    def _():
        wbf_ref[:, 0:tg] = wg_ref[...].astype(BF16)
        wbf_ref[:, tg:tg + ts] = ws_ref[...].astype(BF16)
        wbf_ref[:, tg + ts:] = wv_ref[...].astype(BF16)

    h = h_ref[...]
    acc_ref[...] = jnp.dot(h, wbf_ref[:, 0:tg], preferred_element_type=F32)
    t0 = theta_ref[0:1, :]
    t1 = theta_ref[1:2, :]
    m = jnp.maximum(t0, t1)
    e0 = jnp.exp(t0 - m)
    e1 = jnp.exp(t1 - m)
    lb = e0 / (e0 + e1)
    f = 0.5 * (1.0 + lb) + (0.5 * (1.0 - lb)) * jnp.tanh(0.5 * acc_ref[...])
    lf = jnp.log(f)
    hi = lf.astype(BF16)
    _store_heads(hi_ref, hi)
    _store_heads(lo_ref, lf - hi.astype(F32))
    _store_heads(k_ref, 1.0 - f)
    acc = jnp.dot(h, wbf_ref[:, tg:tg + ts], preferred_element_type=F32)
    _store_heads(s_ref, acc * _sigmoid(acc))
    _store_heads(v_ref, jnp.dot(h, wbf_ref[:, tg + ts:], preferred_element_type=F32))


def _inproj_hgrn(hn, w, theta2, gate_col_start, silu_col_starts, value_col_start, n_cols,
                 tm=1024, tg=512):
    t, d = hn.shape
    n_blocks = 2 * n_cols // tg
    half = n_blocks // 2
    tv = n_cols // n_blocks
    g0 = gate_col_start // tg
    s0, s1 = (cs // tg for cs in silu_col_starts)
    v0 = value_col_start // tv

    def out(width, n_slices):
        return (pl.BlockSpec((width // LANES, tm, LANES), lambda c, i: (c, i, 0)),
                jax.ShapeDtypeStruct((n_slices * n_cols // LANES, t, LANES), BF16))

    outs = [out(tg, 2)] * 4 + [out(tv, 1)]
    return pl.pallas_call(
        _inproj_hgrn_kernel,
        grid=(n_blocks, t // tm),
        in_specs=[pl.BlockSpec((tm, d), lambda c, i: (i, 0)),
                  pl.BlockSpec((d, tg), lambda c, i: (0, g0 + c)),
                  pl.BlockSpec((d, tg),
                               lambda c, i: (0, jnp.where(c < half, s0 + c, s1 + c - half))),
                  pl.BlockSpec((d, tv), lambda c, i: (0, v0 + c)),
                  pl.BlockSpec((theta2.shape[0], tg), lambda c, i: (0, c))],
        out_specs=[o[0] for o in outs],
        out_shape=[o[1] for o in outs],
        scratch_shapes=[pltpu.VMEM((d, 2 * tg + tv), BF16), pltpu.VMEM((tm, tg), F32)],
        compiler_params=pltpu.CompilerParams(
            dimension_semantics=("arbitrary", "arbitrary"),
            vmem_limit_bytes=VMEM_LIMIT_BYTES),
        name="inproj_hgrn",
    )(hn, w, w, w, theta2)


def _convproj_kernel(h_ref, wb_ref, wc_ref, wh_ref, wz_ref, cw_ref, wo_ref, wg_ref, wp_ref,
                     y_ref, zs_ref, wo_bf_ref, wg_bf_ref, wp_bf_ref, wbf_ref):
    seq = h_ref.shape[0]
    wo_bf_ref[...] = wo_ref[...].astype(BF16)
    wg_bf_ref[...] = wg_ref[...].astype(BF16)
    wp_bf_ref[...] = wp_ref[...].astype(BF16)

    @pl.when(pl.program_id(1) == 0)
    def _():
        for k, w_ref in enumerate((wb_ref, wc_ref, wh_ref, wz_ref)):
            wbf_ref[k] = w_ref[...].astype(BF16)

    h = h_ref[...]
    cg = jnp.dot(h, wbf_ref[1], preferred_element_type=F32)
    hb = jnp.dot(h, wbf_ref[2], preferred_element_type=F32)
    u = cg * hb
    row = lax.broadcasted_iota(jnp.int32, u.shape, 0)
    u_prev = jnp.where(row == 0, 0.0, pltpu.roll(u, 1, axis=0))
    u_next = jnp.where(row == seq - 1, 0.0, pltpu.roll(u, seq - 1, axis=0))
    conv = cw_ref[0:1, :] * u_prev + cw_ref[1:2, :] * u + cw_ref[2:3, :] * u_next
    zb = jnp.dot(h, wbf_ref[3], preferred_element_type=F32)
    zs_ref[...] = (zb * _sigmoid(zb)).astype(zs_ref.dtype)
    bg = jnp.dot(h, wbf_ref[0], preferred_element_type=F32)
    y_ref[...] = (bg * conv).astype(y_ref.dtype)


def _convproj(hn, w, cw, col_start, seq, n_ch, out_weights, tnc=256):
    t, d = hn.shape
    batch = t // seq
    blk0 = col_start // tnc
    per_slice = n_ch // tnc
    n_steps = per_slice * batch

    def w_spec(k):
        return pl.BlockSpec((d, tnc), lambda c, b, k=k: (0, blk0 + k * per_slice + c))

    def slab_spec(a):
        rows = max(a.shape[0] // n_steps, BF16_SUBLANES)
        last = a.shape[0] // rows - 1
        return pl.BlockSpec((rows, a.shape[1]),
                            lambda c, b: (jnp.minimum(c * batch + b, last), 0))

    slabs = [slab_spec(a) for a in out_weights]
    out_spec = pl.BlockSpec((seq, tnc), lambda c, b: (b, c))
    out_sds = jax.ShapeDtypeStruct((t, n_ch), BF16)
    return pl.pallas_call(
        _convproj_kernel,
        grid=(per_slice, batch),
        in_specs=[pl.BlockSpec((seq, d), lambda c, b: (b, 0)),
                  w_spec(0), w_spec(1), w_spec(2), w_spec(3),
                  pl.BlockSpec((cw.shape[0], tnc), lambda c, b: (0, c))] + slabs,
        out_specs=[out_spec, out_spec] + slabs,
        out_shape=[out_sds, out_sds] + [jax.ShapeDtypeStruct(a.shape, BF16)
                                        for a in out_weights],
        scratch_shapes=[pltpu.VMEM((4, d, tnc), BF16)],
        compiler_params=pltpu.CompilerParams(
            dimension_semantics=("arbitrary", "arbitrary"),
            vmem_limit_bytes=VMEM_LIMIT_BYTES),
        name="convproj",
    )(hn, w, w, w, w, cw, *out_weights)


def _hgrn_kernel(qs_ref, v_ref, hif_ref, lof_ref, kf_ref, hib_ref, lob_ref, kb_ref, zs_ref,
                 nw_ref, o_ref, acc_ref, qin_ref, kv_ref, dec_ref, sprev_ref):
    seq = qs_ref.shape[1]
    n_chunks = seq // CHUNK
    head = pl.program_id(1)
    c = CHUNK
    mid = c // 2 - 1

    row = lax.broadcasted_iota(jnp.int32, (c, c), 0)
    col = lax.broadcasted_iota(jnp.int32, (c, c), 1)
    masks = (col <= row, col >= row)
    row2 = lax.broadcasted_iota(jnp.int32, (c, 2 * c), 0)
    col2 = lax.broadcasted_iota(jnp.int32, (c, 2 * c), 1) % c
    tris = ((col2 <= row2).astype(BF16), (col2 >= row2).astype(BF16))
    hi_refs = (hif_ref, hib_ref)
    lo_refs = (lof_ref, lob_ref)
    k_refs = (kf_ref, kb_ref)
    mid_rows = (mid, c - 1 - mid)
    last_rows = (c - 1, 0)

    gsz = GROUP * c
    n_groups = n_chunks // GROUP

    def rows(a, j):
        return a[j * c:(j + 1) * c]

    def intra(g, carry):
        r = pl.ds(pl.multiple_of(g * gsz, gsz), gsz)
        qs = qs_ref[0, r, :].astype(F32)
        v = v_ref[0, r, :]
        ks, bs = [], []
        for d in range(2):
            hi = hi_refs[d][0, r, :]
            lo = lo_refs[d][0, r, :]
            ks.append(k_refs[d][0, r, :].astype(F32))
            bs.append([_dot(tris[d], jnp.concatenate([rows(hi, j), rows(lo, j)], axis=0))
                       for j in range(GROUP)])
        ss = ([], [])
        for j in range(GROUP):
            n = g * GROUP + j
            rj = pl.ds(pl.multiple_of(n * c, c), c)
            for d in range(2):
                b = bs[d][j]
                b_mid = b[mid_rows[d]:mid_rows[d] + 1, :]
                b_last = b[last_rows[d]:last_rows[d] + 1, :]
                q_rel = rows(qs, j) * jnp.exp(b - b_mid)
                k_rel = rows(ks[d], j) * jnp.exp(b_mid - b)
                ss[d].append(_dot(q_rel.astype(BF16), k_rel.astype(BF16), ((1,), (1,))))
                q_in = q_rel * jnp.exp(b_mid)
                k_st = k_rel * jnp.exp(b_last - b_mid)
                qin_ref[d, rj, :] = q_in.astype(BF16)
                kv_ref[d, n] = _dot(rows(v, j), k_st.astype(BF16), ((0,), (0,)))
                dec_ref[d, n] = jnp.broadcast_to(jnp.exp(b_last), (8, LANES))
        for j in range(GROUP):
            rj = pl.ds(pl.multiple_of((g * GROUP + j) * c, c), c)
            p = jnp.where(masks[0], ss[0][j], 0.0) + jnp.where(masks[1], ss[1][j], 0.0)
            acc_ref[rj, :] = _dot(p.astype(BF16), rows(v, j))
        return carry

    lax.fori_loop(0, n_groups, intra, 0)

    def scan(i, sts):
        st_f, st_b = sts
        nb = n_chunks - 1 - i
        sprev_ref[i, :, 0:LANES] = st_f.astype(BF16)
        sprev_ref[nb, :, LANES:2 * LANES] = st_b.astype(BF16)
        return (st_f * dec_ref[0, i][0:1, :] + kv_ref[0, i],
                st_b * dec_ref[1, nb][0:1, :] + kv_ref[1, nb])

    zero = jnp.zeros((LANES, LANES), F32)
    lax.fori_loop(0, n_chunks, scan, (zero, zero))

    nw = nw_ref[pl.ds(head, 1), :]

    def finish(g, carry):
        r = pl.ds(pl.multiple_of(g * gsz, gsz), gsz)
        inter = []
        for j in range(GROUP):
            n = g * GROUP + j
            rj = pl.ds(pl.multiple_of(n * c, c), c)
            qin = jnp.concatenate([qin_ref[0, rj, :], qin_ref[1, rj, :]], axis=1)
            inter.append(_dot(qin, sprev_ref[n], ((1,), (1,))))
        o = acc_ref[r, :] + jnp.concatenate(inter, axis=0)
        ms = jnp.mean(o * o, axis=-1, keepdims=True)
        o_ref[r, :] = (o * lax.rsqrt(ms + EPS) * nw
                       * zs_ref[0, r, :].astype(F32)).astype(o_ref.dtype)
        return carry

    lax.fori_loop(0, n_groups, finish, 0)


def _hgrn(sil, v, hi, lo, k, nw, batch, seq, n_heads):
    t = batch * seq
    n_chunks = seq // CHUNK

    def spec(first):
        return pl.BlockSpec((1, seq, LANES), lambda b, h: (first + h, b, 0))

    fwd, bwd = spec(0), spec(n_heads)
    return pl.pallas_call(
        _hgrn_kernel,
        grid=(batch, n_heads),
        in_specs=[fwd, fwd, fwd, fwd, fwd, bwd, bwd, bwd, bwd,
                  pl.BlockSpec(nw.shape, lambda b, h: (0, 0))],
        out_specs=pl.BlockSpec((seq, LANES), lambda b, h: (b, h)),
        out_shape=jax.ShapeDtypeStruct((t, n_heads * LANES), BF16),
        scratch_shapes=[
            pltpu.VMEM((seq, LANES), F32),
            pltpu.VMEM((2, seq, LANES), BF16),
            pltpu.VMEM((2, n_chunks, LANES, LANES), F32),
            pltpu.VMEM((2, n_chunks, 8, LANES), F32),
            pltpu.VMEM((n_chunks, LANES, 2 * LANES), BF16),
        ],
        compiler_params=pltpu.CompilerParams(
            dimension_semantics=("arbitrary", "arbitrary"),
            vmem_limit_bytes=VMEM_LIMIT_BYTES),
        name="hgrn2",
    )(sil, v, hi, lo, k, hi, lo, k, sil, nw)


def _out_kernel(oa_ref, y_ref, zs_ref, x_ref, p_ref, wo_ref, wg_ref, wp_ref, cnw_ref, fw_ref,
                o_ref):
    d_a = oa_ref.shape[1]
    y = y_ref[...].astype(F32)
    ms_b = jnp.mean(y * y, axis=-1, keepdims=True)
    ob = (y * zs_ref[...].astype(F32) * (lax.rsqrt(ms_b + EPS) * cnw_ref[...])).astype(BF16)
    h1 = (x_ref[...]
          + jnp.dot(oa_ref[...], wo_ref[0:d_a, :], preferred_element_type=F32)
          + jnp.dot(ob, wo_ref[d_a:, :], preferred_element_type=F32))
    gate = _sigmoid(jnp.dot(h1.astype(BF16), wg_ref[...], preferred_element_type=F32))
    ple = jnp.dot(p_ref[...].astype(BF16), wp_ref[...], preferred_element_type=F32)
    h2 = h1 + ple * gate
    ms = jnp.mean(h2 * h2, axis=-1, keepdims=True)
    o_ref[...] = h2 * lax.rsqrt(ms + EPS) * fw_ref[...]


def _out(oa, y, zs, x2d, p2d, wo, wg, wp, cnw, fw, tm=256):
    t, d = x2d.shape

    def resident(shape):
        return pl.BlockSpec(shape, lambda i: (0, 0), pipeline_mode=pl.Buffered(1))

    def rows(a):
        return pl.BlockSpec((tm, a.shape[1]), lambda i: (i, 0))

    return pl.pallas_call(
        _out_kernel,
        grid=(t // tm,),
        in_specs=[rows(oa), rows(y), rows(zs), rows(x2d), rows(p2d),
                  resident(wo.shape), resident(wg.shape), resident(wp.shape),
                  resident((1, d)), resident((1, d))],
        out_specs=pl.BlockSpec((tm, d), lambda i: (i, 0)),
        out_shape=jax.ShapeDtypeStruct((t, d), F32),
        compiler_params=pltpu.CompilerParams(
            dimension_semantics=("arbitrary",),
            vmem_limit_bytes=VMEM_LIMIT_BYTES),
        name="outproj",
    )(oa, y, zs, x2d, p2d, wo, wg, wp, cnw.reshape(1, d), fw.reshape(1, d))


def kernel(x, p, norm_w, w_in, lb_theta, hgrn_norm_w, conv_w, conv_norm_w, w_out, w_ple,
           w_ple_gate, final_norm_w):
    batch, seq, d = x.shape
    depth = p.shape[0]
    assert depth == 1 and lb_theta.shape[1] == 2
    t = batch * seq
    n_heads = d // LANES

    x2d = x.reshape(t, d)
    hn = _rmsnorm(x2d, norm_w[0])
    w = w_in[0]

    theta2 = jnp.transpose(lb_theta, (1, 0, 2)).reshape(2, 2 * d)
    hi, lo, k, sil, v = _inproj_hgrn(hn, w, theta2, 2 * d, (0 * d, 4 * d), 1 * d, d)
    o_a = _hgrn(sil, v, hi, lo, k, hgrn_norm_w[0].reshape(n_heads, LANES), batch, seq, n_heads)

    y_b, zs_b, wo, wg, wp = _convproj(hn, w, conv_w[0], 5 * d, seq, d,
                                      (w_out[0], w_ple_gate[0], w_ple[0]))

    out = _out(o_a, y_b, zs_b, x2d, p[0].reshape(t, -1), wo, wg, wp,
               conv_norm_w[0], final_norm_w)
    return out.reshape(batch, seq, d)
```

```python
import jax
import jax.numpy as jnp
from jax import lax
from jax.experimental import pallas as pl
from jax.experimental.pallas import tpu as pltpu

EPS = 1e-6
LANES = 128
BF16_SUBLANES = 16
CHUNK = 64
GROUP = 32
VMEM_CAPACITY_BYTES = 64 * 1024 * 1024
VMEM_LIMIT_BYTES = VMEM_CAPACITY_BYTES - 4 * 1024 * 1024

F32 = jnp.float32
BF16 = jnp.bfloat16


def _sigmoid(x):
    return 0.5 * jnp.tanh(0.5 * x) + 0.5


def _dot(a, b, dims=((1,), (0,))):
    return lax.dot_general(a, b, (dims, ((), ())), preferred_element_type=F32)


def _rmsnorm_kernel(x_ref, w_ref, o_ref):
    x = x_ref[...]
    ms = jnp.mean(x * x, axis=-1, keepdims=True)
    o_ref[...] = (x * lax.rsqrt(ms + EPS) * w_ref[...]).astype(o_ref.dtype)


def _rmsnorm(x2d, w, tm=512):
    t, d = x2d.shape
    return pl.pallas_call(
        _rmsnorm_kernel,
        grid=(t // tm,),
        in_specs=[pl.BlockSpec((tm, d), lambda i: (i, 0)),
                  pl.BlockSpec((1, d), lambda i: (0, 0))],
        out_specs=pl.BlockSpec((tm, d), lambda i: (i, 0)),
        out_shape=jax.ShapeDtypeStruct((t, d), BF16),
        name="rmsnorm",
    )(x2d, w.reshape(1, d))


def _store_heads(o_ref, first, val):
    for g in range(val.shape[1] // LANES):
        o_ref[first + g] = val[:, g * LANES:(g + 1) * LANES].astype(o_ref.dtype)


def _inproj_hgrn_kernel(h_ref, wg_ref, ws_ref, wv_ref, theta_ref, o_ref, wbf_ref, acc_ref):
    tg = wg_ref.shape[1]
    ts = ws_ref.shape[1]
    ng, ns = tg // LANES, ts // LANES

    @pl.when(pl.program_id(1) == 0)
    def _():
        wbf_ref[:, 0:tg] = wg_ref[...].astype(BF16)
        wbf_ref[:, tg:tg + ts] = ws_ref[...].astype(BF16)
        wbf_ref[:, tg + ts:] = wv_ref[...].astype(BF16)

    h = h_ref[...]
    acc_ref[...] = jnp.dot(h, wbf_ref[:, 0:tg], preferred_element_type=F32)
    t0 = theta_ref[0:1, :]
    t1 = theta_ref[1:2, :]
    m = jnp.maximum(t0, t1)
    e0 = jnp.exp(t0 - m)
    e1 = jnp.exp(t1 - m)
    lb = e0 / (e0 + e1)
    f = 0.5 * (1.0 + lb) + (0.5 * (1.0 - lb)) * jnp.tanh(0.5 * acc_ref[...])
    lf = jnp.log(f)
    hi = lf.astype(BF16)
    _store_heads(o_ref, 0, hi)
    _store_heads(o_ref, ng, lf - hi.astype(F32))
    _store_heads(o_ref, 2 * ng, 1.0 - f)
    acc = jnp.dot(h, wbf_ref[:, tg:tg + ts], preferred_element_type=F32)
    _store_heads(o_ref, 3 * ng, acc * _sigmoid(acc))
    _store_heads(o_ref, 3 * ng + ns,
                 jnp.dot(h, wbf_ref[:, tg + ts:], preferred_element_type=F32))


def _inproj_hgrn(hn, w, theta2, gate_col_start, silu_col_starts, value_col_start, n_cols,
                 tm=1024, tg=512):
    t, d = hn.shape
    n_blocks = 2 * n_cols // tg
    half = n_blocks // 2
    tv = n_cols // n_blocks
    g0 = gate_col_start // tg
    s0, s1 = (cs // tg for cs in silu_col_starts)
    v0 = value_col_start // tv
    slabs = (4 * tg + tv) // LANES
    return slabs, pl.pallas_call(
        _inproj_hgrn_kernel,
        grid=(n_blocks, t // tm),
        in_specs=[pl.BlockSpec((tm, d), lambda c, i: (i, 0)),
                  pl.BlockSpec((d, tg), lambda c, i: (0, g0 + c)),
                  pl.BlockSpec((d, tg),
                               lambda c, i: (0, jnp.where(c < half, s0 + c, s1 + c - half))),
                  pl.BlockSpec((d, tv), lambda c, i: (0, v0 + c)),
                  pl.BlockSpec((theta2.shape[0], tg), lambda c, i: (0, c))],
        out_specs=pl.BlockSpec((slabs, tm, LANES), lambda c, i: (c, i, 0)),
        out_shape=jax.ShapeDtypeStruct((n_blocks * slabs, t, LANES), BF16),
        scratch_shapes=[pltpu.VMEM((d, 2 * tg + tv), BF16), pltpu.VMEM((tm, tg), F32)],
        compiler_params=pltpu.CompilerParams(
            dimension_semantics=("arbitrary", "arbitrary"),
            vmem_limit_bytes=VMEM_LIMIT_BYTES),
        name="inproj_hgrn",
    )(hn, w, w, w, theta2)


def _convproj_kernel(h_ref, wb_ref, wc_ref, wh_ref, wz_ref, cw_ref, wo_ref, wg_ref, wp_ref,
                     y_ref, zs_ref, wo_bf_ref, wg_bf_ref, wp_bf_ref, wbf_ref):
    seq = h_ref.shape[0]
    wo_bf_ref[...] = wo_ref[...].astype(BF16)
    wg_bf_ref[...] = wg_ref[...].astype(BF16)
    wp_bf_ref[...] = wp_ref[...].astype(BF16)

    @pl.when(pl.program_id(1) == 0)
    def _():
        for k, w_ref in enumerate((wb_ref, wc_ref, wh_ref, wz_ref)):
            wbf_ref[k] = w_ref[...].astype(BF16)

    h = h_ref[...]
    cg = jnp.dot(h, wbf_ref[1], preferred_element_type=F32)
    hb = jnp.dot(h, wbf_ref[2], preferred_element_type=F32)
    u = cg * hb
    row = lax.broadcasted_iota(jnp.int32, u.shape, 0)
    u_prev = jnp.where(row == 0, 0.0, pltpu.roll(u, 1, axis=0))
    u_next = jnp.where(row == seq - 1, 0.0, pltpu.roll(u, seq - 1, axis=0))
    conv = cw_ref[0:1, :] * u_prev + cw_ref[1:2, :] * u + cw_ref[2:3, :] * u_next
    zb = jnp.dot(h, wbf_ref[3], preferred_element_type=F32)
    zs_ref[...] = (zb * _sigmoid(zb)).astype(zs_ref.dtype)
    bg = jnp.dot(h, wbf_ref[0], preferred_element_type=F32)
    y_ref[...] = (bg * conv).astype(y_ref.dtype)


def _convproj(hn, w, cw, col_start, seq, n_ch, out_weights, tnc=256):
    t, d = hn.shape
    batch = t // seq
    blk0 = col_start // tnc
    per_slice = n_ch // tnc
    n_steps = per_slice * batch

    def w_spec(k):
        return pl.BlockSpec((d, tnc), lambda c, b, k=k: (0, blk0 + k * per_slice + c))

    def slab_spec(a):
        rows = max(a.shape[0] // n_steps, BF16_SUBLANES)
        last = a.shape[0] // rows - 1
        return pl.BlockSpec((rows, a.shape[1]),
                            lambda c, b: (jnp.minimum(c * batch + b, last), 0))

    slabs = [slab_spec(a) for a in out_weights]
    out_spec = pl.BlockSpec((seq, tnc), lambda c, b: (b, c))
    out_sds = jax.ShapeDtypeStruct((t, n_ch), BF16)
    return pl.pallas_call(
        _convproj_kernel,
        grid=(per_slice, batch),
        in_specs=[pl.BlockSpec((seq, d), lambda c, b: (b, 0)),
                  w_spec(0), w_spec(1), w_spec(2), w_spec(3),
                  pl.BlockSpec((cw.shape[0], tnc), lambda c, b: (0, c))] + slabs,
        out_specs=[out_spec, out_spec] + slabs,
        out_shape=[out_sds, out_sds] + [jax.ShapeDtypeStruct(a.shape, BF16)
                                        for a in out_weights],
        scratch_shapes=[pltpu.VMEM((4, d, tnc), BF16)],
        compiler_params=pltpu.CompilerParams(
            dimension_semantics=("arbitrary", "arbitrary"),
            vmem_limit_bytes=VMEM_LIMIT_BYTES),
        name="convproj",
    )(hn, w, w, w, w, cw, *out_weights)


def _hgrn_kernel(qs_ref, v_ref, hif_ref, lof_ref, kf_ref, hib_ref, lob_ref, kb_ref, zs_ref,
                 nw_ref, o_ref, acc_ref, qin_ref, kv_ref, dec_ref, sprev_ref):
    seq = qs_ref.shape[1]
    n_chunks = seq // CHUNK
    head = pl.program_id(1)
    c = CHUNK
    mid = c // 2 - 1

    row = lax.broadcasted_iota(jnp.int32, (c, c), 0)
    col = lax.broadcasted_iota(jnp.int32, (c, c), 1)
    masks = (col <= row, col >= row)
    row2 = lax.broadcasted_iota(jnp.int32, (c, 2 * c), 0)
    col2 = lax.broadcasted_iota(jnp.int32, (c, 2 * c), 1) % c
    tris = ((col2 <= row2).astype(BF16), (col2 >= row2).astype(BF16))
    hi_refs = (hif_ref, hib_ref)
    lo_refs = (lof_ref, lob_ref)
    k_refs = (kf_ref, kb_ref)
    mid_rows = (mid, c - 1 - mid)
    last_rows = (c - 1, 0)

    gsz = GROUP * c
    n_groups = n_chunks // GROUP

    def rows(a, j):
        return a[j * c:(j + 1) * c]

    def intra(g, carry):
        r = pl.ds(pl.multiple_of(g * gsz, gsz), gsz)
        qs = qs_ref[0, r, :].astype(F32)
        v = v_ref[0, r, :]
        ks, bs = [], []
        for d in range(2):
            hi = hi_refs[d][0, r, :]
            lo = lo_refs[d][0, r, :]
            ks.append(k_refs[d][0, r, :].astype(F32))
            bs.append([_dot(tris[d], jnp.concatenate([rows(hi, j), rows(lo, j)], axis=0))
                       for j in range(GROUP)])
        ss = ([], [])
        for j in range(GROUP):
            n = g * GROUP + j
            rj = pl.ds(pl.multiple_of(n * c, c), c)
            for d in range(2):
                b = bs[d][j]
                b_mid = b[mid_rows[d]:mid_rows[d] + 1, :]
                b_last = b[last_rows[d]:last_rows[d] + 1, :]
                q_rel = rows(qs, j) * jnp.exp(b - b_mid)
                k_rel = rows(ks[d], j) * jnp.exp(b_mid - b)
                ss[d].append(_dot(q_rel.astype(BF16), k_rel.astype(BF16), ((1,), (1,))))
                q_in = q_rel * jnp.exp(b_mid)
                k_st = k_rel * jnp.exp(b_last - b_mid)
                qin_ref[d, rj, :] = q_in.astype(BF16)
                kv_ref[d, n] = _dot(rows(v, j), k_st.astype(BF16), ((0,), (0,)))
                dec_ref[d, n] = jnp.broadcast_to(jnp.exp(b_last), (8, LANES))
        for j in range(GROUP):
            rj = pl.ds(pl.multiple_of((g * GROUP + j) * c, c), c)
            p = jnp.where(masks[0], ss[0][j], 0.0) + jnp.where(masks[1], ss[1][j], 0.0)
            acc_ref[rj, :] = _dot(p.astype(BF16), rows(v, j))
        return carry

    lax.fori_loop(0, n_groups, intra, 0)

    def scan(i, sts):
        st_f, st_b = sts
        nb = n_chunks - 1 - i
        sprev_ref[i, :, 0:LANES] = st_f.astype(BF16)
        sprev_ref[nb, :, LANES:2 * LANES] = st_b.astype(BF16)
        return (st_f * dec_ref[0, i][0:1, :] + kv_ref[0, i],
                st_b * dec_ref[1, nb][0:1, :] + kv_ref[1, nb])

    zero = jnp.zeros((LANES, LANES), F32)
    lax.fori_loop(0, n_chunks, scan, (zero, zero))

    nw = nw_ref[pl.ds(head, 1), :]

    def finish(g, carry):
        r = pl.ds(pl.multiple_of(g * gsz, gsz), gsz)
        inter = []
        for j in range(GROUP):
            n = g * GROUP + j
            rj = pl.ds(pl.multiple_of(n * c, c), c)
            qin = jnp.concatenate([qin_ref[0, rj, :], qin_ref[1, rj, :]], axis=1)
            inter.append(_dot(qin, sprev_ref[n], ((1,), (1,))))
        o = acc_ref[r, :] + jnp.concatenate(inter, axis=0)
        ms = jnp.mean(o * o, axis=-1, keepdims=True)
        o_ref[r, :] = (o * lax.rsqrt(ms + EPS) * nw
                       * zs_ref[0, r, :].astype(F32)).astype(o_ref.dtype)
        return carry

    lax.fori_loop(0, n_groups, finish, 0)


def _hgrn(proj, slabs, nw, batch, seq, n_heads):
    t = batch * seq
    n_chunks = seq // CHUNK
    n_blocks = proj.shape[0] // slabs
    half = n_blocks // 2
    ng = n_heads // half
    nv = n_heads // n_blocks

    def gate_spec(kind, second_half):
        return pl.BlockSpec(
            (1, seq, LANES),
            lambda b, h: ((second_half * half + h // ng) * slabs + kind * ng + h % ng, b, 0))

    value_spec = pl.BlockSpec(
        (1, seq, LANES), lambda b, h: ((h // nv) * slabs + 4 * ng + h % nv, b, 0))
    return pl.pallas_call(
        _hgrn_kernel,
        grid=(batch, n_heads),
        in_specs=[gate_spec(3, 0), value_spec,
                  gate_spec(0, 0), gate_spec(1, 0), gate_spec(2, 0),
                  gate_spec(0, 1), gate_spec(1, 1), gate_spec(2, 1),
                  gate_spec(3, 1),
                  pl.BlockSpec(nw.shape, lambda b, h: (0, 0))],
        out_specs=pl.BlockSpec((seq, LANES), lambda b, h: (b, h)),
        out_shape=jax.ShapeDtypeStruct((t, n_heads * LANES), BF16),
        scratch_shapes=[
            pltpu.VMEM((seq, LANES), F32),
            pltpu.VMEM((2, seq, LANES), BF16),
            pltpu.VMEM((2, n_chunks, LANES, LANES), F32),
            pltpu.VMEM((2, n_chunks, 8, LANES), F32),
            pltpu.VMEM((n_chunks, LANES, 2 * LANES), BF16),
        ],
        compiler_params=pltpu.CompilerParams(
            dimension_semantics=("arbitrary", "arbitrary"),
            vmem_limit_bytes=VMEM_LIMIT_BYTES),
        name="hgrn2",
    )(*([proj] * 9), nw)


def _out_kernel(oa_ref, y_ref, zs_ref, x_ref, p_ref, wo_ref, wg_ref, wp_ref, cnw_ref, fw_ref,
                o_ref):
    d_a = oa_ref.shape[1]
    y = y_ref[...].astype(F32)
    ms_b = jnp.mean(y * y, axis=-1, keepdims=True)
    ob = (y * zs_ref[...].astype(F32) * (lax.rsqrt(ms_b + EPS) * cnw_ref[...])).astype(BF16)
    h1 = (x_ref[...]
          + jnp.dot(oa_ref[...], wo_ref[0:d_a, :], preferred_element_type=F32)
          + jnp.dot(ob, wo_ref[d_a:, :], preferred_element_type=F32))
    gate = _sigmoid(jnp.dot(h1.astype(BF16), wg_ref[...], preferred_element_type=F32))
    ple = jnp.dot(p_ref[...].astype(BF16), wp_ref[...], preferred_element_type=F32)
    h2 = h1 + ple * gate
    ms = jnp.mean(h2 * h2, axis=-1, keepdims=True)
    o_ref[...] = h2 * lax.rsqrt(ms + EPS) * fw_ref[...]


def _out(oa, y, zs, x2d, p2d, wo, wg, wp, cnw, fw, tm=256):
    t, d = x2d.shape

    def resident(shape):
        return pl.BlockSpec(shape, lambda i: (0, 0), pipeline_mode=pl.Buffered(1))

    def rows(a):
        return pl.BlockSpec((tm, a.shape[1]), lambda i: (i, 0))

    return pl.pallas_call(
        _out_kernel,
        grid=(t // tm,),
        in_specs=[rows(oa), rows(y), rows(zs), rows(x2d), rows(p2d),
                  resident(wo.shape), resident(wg.shape), resident(wp.shape),
                  resident((1, d)), resident((1, d))],
        out_specs=pl.BlockSpec((tm, d), lambda i: (i, 0)),
        out_shape=jax.ShapeDtypeStruct((t, d), F32),
        compiler_params=pltpu.CompilerParams(
            dimension_semantics=("arbitrary",),
            vmem_limit_bytes=VMEM_LIMIT_BYTES),
        name="outproj",
    )(oa, y, zs, x2d, p2d, wo, wg, wp, cnw.reshape(1, d), fw.reshape(1, d))


def kernel(x, p, norm_w, w_in, lb_theta, hgrn_norm_w, conv_w, conv_norm_w, w_out, w_ple,
           w_ple_gate, final_norm_w):
    batch, seq, d = x.shape
    depth = p.shape[0]
    assert depth == 1 and lb_theta.shape[1] == 2
    t = batch * seq
    n_heads = d // LANES

    x2d = x.reshape(t, d)
    hn = _rmsnorm(x2d, norm_w[0])
    w = w_in[0]

    theta2 = jnp.transpose(lb_theta, (1, 0, 2)).reshape(2, 2 * d)
    slabs, proj_a = _inproj_hgrn(hn, w, theta2, 2 * d, (0 * d, 4 * d), 1 * d, d)
    o_a = _hgrn(proj_a, slabs, hgrn_norm_w[0].reshape(n_heads, LANES), batch, seq, n_heads)

    y_b, zs_b, wo, wg, wp = _convproj(hn, w, conv_w[0], 5 * d, seq, d,
                                      (w_out[0], w_ple_gate[0], w_ple[0]))

    out = _out(o_a, y_b, zs_b, x2d, p[0].reshape(t, -1), wo, wg, wp,
               conv_norm_w[0], final_norm_w)
    return out.reshape(batch, seq, d)
```

```python
import jax
import jax.numpy as jnp
from jax import lax
from jax.experimental import pallas as pl
from jax.experimental.pallas import tpu as pltpu

EPS = 1e-6
LANES = 128
BF16_SUBLANES = 16
CHUNK = 64
HEADS_PER_STEP = 2
VMEM_CAPACITY_BYTES = 64 * 1024 * 1024
VMEM_LIMIT_BYTES = VMEM_CAPACITY_BYTES - 4 * 1024 * 1024

F32 = jnp.float32
BF16 = jnp.bfloat16


def _sigmoid(x):
    return 0.5 * jnp.tanh(0.5 * x) + 0.5


def _dot(a, b, dims=((1,), (0,))):
    return lax.dot_general(a, b, (dims, ((), ())), preferred_element_type=F32)


def _rmsnorm_kernel(x_ref, w_ref, o_ref):
    x = x_ref[...]
    ms = jnp.mean(x * x, axis=-1, keepdims=True)
    o_ref[...] = (x * lax.rsqrt(ms + EPS) * w_ref[...]).astype(o_ref.dtype)


def _rmsnorm(x2d, w, tm=512):
    t, d = x2d.shape
    return pl.pallas_call(
        _rmsnorm_kernel,
        grid=(t // tm,),
        in_specs=[pl.BlockSpec((tm, d), lambda i: (i, 0)),
                  pl.BlockSpec((1, d), lambda i: (0, 0))],
        out_specs=pl.BlockSpec((tm, d), lambda i: (i, 0)),
        out_shape=jax.ShapeDtypeStruct((t, d), BF16),
        name="rmsnorm",
    )(x2d, w.reshape(1, d))


def _store_heads(o_ref, first, val):
    for g in range(val.shape[1] // LANES):
        o_ref[first + g] = val[:, g * LANES:(g + 1) * LANES].astype(o_ref.dtype)


def _inproj_hgrn_kernel(h_ref, wg_ref, ws_ref, wv_ref, theta_ref, o_ref, wbf_ref, acc_ref):
    tg = wg_ref.shape[1]
    ts = ws_ref.shape[1]
    ng, ns = tg // LANES, ts // LANES

    @pl.when(pl.program_id(1) == 0)
    def _():
        wbf_ref[:, 0:tg] = wg_ref[...].astype(BF16)
        wbf_ref[:, tg:tg + ts] = ws_ref[...].astype(BF16)
        wbf_ref[:, tg + ts:] = wv_ref[...].astype(BF16)

    h = h_ref[...]
    acc_ref[...] = jnp.dot(h, wbf_ref[:, 0:tg], preferred_element_type=F32)
    t0 = theta_ref[0:1, :]
    t1 = theta_ref[1:2, :]
    m = jnp.maximum(t0, t1)
    e0 = jnp.exp(t0 - m)
    e1 = jnp.exp(t1 - m)
    lb = e0 / (e0 + e1)
    f = 0.5 * (1.0 + lb) + (0.5 * (1.0 - lb)) * jnp.tanh(0.5 * acc_ref[...])
    lf = jnp.log(f)
    hi = lf.astype(BF16)
    _store_heads(o_ref, 0, hi)
    _store_heads(o_ref, ng, lf - hi.astype(F32))
    _store_heads(o_ref, 2 * ng, 1.0 - f)
    acc = jnp.dot(h, wbf_ref[:, tg:tg + ts], preferred_element_type=F32)
    _store_heads(o_ref, 3 * ng, acc * _sigmoid(acc))
    _store_heads(o_ref, 3 * ng + ns,
                 jnp.dot(h, wbf_ref[:, tg + ts:], preferred_element_type=F32))


def _inproj_hgrn(hn, w, theta2, gate_col_start, silu_col_starts, value_col_start, n_cols,
                 tm=1024, tg=512):
    t, d = hn.shape
    n_blocks = 2 * n_cols // tg
    half = n_blocks // 2
    tv = n_cols // n_blocks
    g0 = gate_col_start // tg
    s0, s1 = (cs // tg for cs in silu_col_starts)
    v0 = value_col_start // tv
    slabs = (4 * tg + tv) // LANES
    return slabs, pl.pallas_call(
        _inproj_hgrn_kernel,
        grid=(n_blocks, t // tm),
        in_specs=[pl.BlockSpec((tm, d), lambda c, i: (i, 0)),
                  pl.BlockSpec((d, tg), lambda c, i: (0, g0 + c)),
                  pl.BlockSpec((d, tg),
                               lambda c, i: (0, jnp.where(c < half, s0 + c, s1 + c - half))),
                  pl.BlockSpec((d, tv), lambda c, i: (0, v0 + c)),
                  pl.BlockSpec((theta2.shape[0], tg), lambda c, i: (0, c))],
        out_specs=pl.BlockSpec((slabs, tm, LANES), lambda c, i: (c, i, 0)),
        out_shape=jax.ShapeDtypeStruct((n_blocks * slabs, t, LANES), BF16),
        scratch_shapes=[pltpu.VMEM((d, 2 * tg + tv), BF16), pltpu.VMEM((tm, tg), F32)],
        compiler_params=pltpu.CompilerParams(
            dimension_semantics=("arbitrary", "arbitrary"),
            vmem_limit_bytes=VMEM_LIMIT_BYTES),
        name="inproj_hgrn",
    )(hn, w, w, w, theta2)


def _convproj_kernel(h_ref, wb_ref, wc_ref, wh_ref, wz_ref, cw_ref, wo_ref, wg_ref, wp_ref,
                     y_ref, zs_ref, wo_bf_ref, wg_bf_ref, wp_bf_ref, wbf_ref):
    seq = h_ref.shape[0]
    wo_bf_ref[...] = wo_ref[...].astype(BF16)
    wg_bf_ref[...] = wg_ref[...].astype(BF16)
    wp_bf_ref[...] = wp_ref[...].astype(BF16)

    @pl.when(pl.program_id(1) == 0)
    def _():
        for k, w_ref in enumerate((wb_ref, wc_ref, wh_ref, wz_ref)):
            wbf_ref[k] = w_ref[...].astype(BF16)

    h = h_ref[...]
    cg = jnp.dot(h, wbf_ref[1], preferred_element_type=F32)
    hb = jnp.dot(h, wbf_ref[2], preferred_element_type=F32)
    u = cg * hb
    row = lax.broadcasted_iota(jnp.int32, u.shape, 0)
    u_prev = jnp.where(row == 0, 0.0, pltpu.roll(u, 1, axis=0))
    u_next = jnp.where(row == seq - 1, 0.0, pltpu.roll(u, seq - 1, axis=0))
    conv = cw_ref[0:1, :] * u_prev + cw_ref[1:2, :] * u + cw_ref[2:3, :] * u_next
    zb = jnp.dot(h, wbf_ref[3], preferred_element_type=F32)
    zs_ref[...] = (zb * _sigmoid(zb)).astype(zs_ref.dtype)
    bg = jnp.dot(h, wbf_ref[0], preferred_element_type=F32)
    y_ref[...] = (bg * conv).astype(y_ref.dtype)


def _convproj(hn, w, cw, col_start, seq, n_ch, out_weights, tnc=256):
    t, d = hn.shape
    batch = t // seq
    blk0 = col_start // tnc
    per_slice = n_ch // tnc
    n_steps = per_slice * batch

    def w_spec(k):
        return pl.BlockSpec((d, tnc), lambda c, b, k=k: (0, blk0 + k * per_slice + c))

    def slab_spec(a):
        rows = max(a.shape[0] // n_steps, BF16_SUBLANES)
        last = a.shape[0] // rows - 1
        return pl.BlockSpec((rows, a.shape[1]),
                            lambda c, b: (jnp.minimum(c * batch + b, last), 0))

    slabs = [slab_spec(a) for a in out_weights]
    out_spec = pl.BlockSpec((seq, tnc), lambda c, b: (b, c))
    out_sds = jax.ShapeDtypeStruct((t, n_ch), BF16)
    return pl.pallas_call(
        _convproj_kernel,
        grid=(per_slice, batch),
        in_specs=[pl.BlockSpec((seq, d), lambda c, b: (b, 0)),
                  w_spec(0), w_spec(1), w_spec(2), w_spec(3),
                  pl.BlockSpec((cw.shape[0], tnc), lambda c, b: (0, c))] + slabs,
        out_specs=[out_spec, out_spec] + slabs,
        out_shape=[out_sds, out_sds] + [jax.ShapeDtypeStruct(a.shape, BF16)
                                        for a in out_weights],
        scratch_shapes=[pltpu.VMEM((4, d, tnc), BF16)],
        compiler_params=pltpu.CompilerParams(
            dimension_semantics=("arbitrary", "arbitrary"),
            vmem_limit_bytes=VMEM_LIMIT_BYTES),
        name="convproj",
    )(hn, w, w, w, w, cw, *out_weights)


def _hgrn_kernel(qs_ref, v_ref, hif_ref, lof_ref, kf_ref, hib_ref, lob_ref, kb_ref, zs_ref,
                 nw_ref, o_ref, acc_ref, qin_ref, kv_ref, dec_ref, sprev_ref):
    seq = qs_ref.shape[1]
    c = CHUNK
    n_chunks = seq // c
    head0 = pl.program_id(1) * HEADS_PER_STEP
    mid = c // 2 - 1

    row = lax.broadcasted_iota(jnp.int32, (c, c), 0)
    col = lax.broadcasted_iota(jnp.int32, (c, c), 1)
    masks = (col <= row, col >= row)
    row2 = lax.broadcasted_iota(jnp.int32, (c, 2 * c), 0)
    col2 = lax.broadcasted_iota(jnp.int32, (c, 2 * c), 1) % c
    tris = ((col2 <= row2).astype(BF16), (col2 >= row2).astype(BF16))
    hi_refs = (hif_ref, hib_ref)
    lo_refs = (lof_ref, lob_ref)
    k_refs = (kf_ref, kb_ref)
    mid_rows = (mid, c - 1 - mid)
    last_rows = (c - 1, 0)

    def rows(j):
        return slice(j * c, (j + 1) * c)

    def intra(hh):
        qs = qs_ref[hh].astype(F32)
        v = v_ref[hh]
        ks, bs = [], []
        for d in range(2):
            hi = hi_refs[d][hh]
            lo = lo_refs[d][hh]
            ks.append(k_refs[d][hh].astype(F32))
            bs.append([_dot(tris[d], jnp.concatenate([hi[rows(j)], lo[rows(j)]], axis=0))
                       for j in range(n_chunks)])
        ss = ([], [])
        for j in range(n_chunks):
            for d in range(2):
                b = bs[d][j]
                b_mid = b[mid_rows[d]:mid_rows[d] + 1, :]
                b_last = b[last_rows[d]:last_rows[d] + 1, :]
                q_rel = qs[rows(j)] * jnp.exp(b - b_mid)
                k_rel = ks[d][rows(j)] * jnp.exp(b_mid - b)
                ss[d].append(_dot(q_rel.astype(BF16), k_rel.astype(BF16), ((1,), (1,))))
                q_in = q_rel * jnp.exp(b_mid)
                k_st = k_rel * jnp.exp(b_last - b_mid)
                qin_ref[hh, d, rows(j), :] = q_in.astype(BF16)
                kv_ref[hh, d, j] = _dot(v[rows(j)], k_st.astype(BF16), ((0,), (0,)))
                dec_ref[hh, d, j] = jnp.broadcast_to(jnp.exp(b_last), (8, LANES))
        for j in range(n_chunks):
            p = jnp.where(masks[0], ss[0][j], 0.0) + jnp.where(masks[1], ss[1][j], 0.0)
            acc_ref[hh, rows(j), :] = _dot(p.astype(BF16), v[rows(j)])

    def scan(hh):
        st_f = jnp.zeros((LANES, LANES), F32)
        st_b = st_f
        for i in range(n_chunks):
            nb = n_chunks - 1 - i
            sprev_ref[hh, i, :, 0:LANES] = st_f.astype(BF16)
            sprev_ref[hh, nb, :, LANES:2 * LANES] = st_b.astype(BF16)
            st_f = st_f * dec_ref[hh, 0, i][0:1, :] + kv_ref[hh, 0, i]
            st_b = st_b * dec_ref[hh, 1, nb][0:1, :] + kv_ref[hh, 1, nb]

    def finish(hh):
        inter = []
        for j in range(n_chunks):
            qin = jnp.concatenate([qin_ref[hh, 0, rows(j), :], qin_ref[hh, 1, rows(j), :]],
                                  axis=1)
            inter.append(_dot(qin, sprev_ref[hh, j], ((1,), (1,))))
        o = acc_ref[hh] + jnp.concatenate(inter, axis=0)
        ms = jnp.mean(o * o, axis=-1, keepdims=True)
        nw = nw_ref[pl.ds(head0 + hh, 1), :]
        o_ref[:, hh * LANES:(hh + 1) * LANES] = (
            o * lax.rsqrt(ms + EPS) * nw * zs_ref[hh].astype(F32)).astype(o_ref.dtype)

    for stage in (intra, scan, finish):
        for hh in range(HEADS_PER_STEP):
            stage(hh)


def _hgrn(proj, slabs, nw, batch, seq, n_heads):
    t = batch * seq
    n_chunks = seq // CHUNK
    hps = HEADS_PER_STEP
    n_blocks = proj.shape[0] // slabs
    half = n_blocks // 2
    ng = n_heads // half
    nv = n_heads // n_blocks
    assert ng % hps == 0 and nv % hps == 0 and slabs % hps == 0

    def gate_spec(kind, second_half):
        return pl.BlockSpec(
            (hps, seq, LANES),
            lambda b, g: (((second_half * half + g * hps // ng) * slabs + kind * ng
                           + g * hps % ng) // hps, b, 0))

    value_spec = pl.BlockSpec(
        (hps, seq, LANES),
        lambda b, g: (((g * hps // nv) * slabs + 4 * ng + g * hps % nv) // hps, b, 0))
    return pl.pallas_call(
        _hgrn_kernel,
        grid=(batch, n_heads // hps),
        in_specs=[gate_spec(3, 0), value_spec,
                  gate_spec(0, 0), gate_spec(1, 0), gate_spec(2, 0),
                  gate_spec(0, 1), gate_spec(1, 1), gate_spec(2, 1),
                  gate_spec(3, 1),
                  pl.BlockSpec(nw.shape, lambda b, g: (0, 0))],
        out_specs=pl.BlockSpec((seq, hps * LANES), lambda b, g: (b, g)),
        out_shape=jax.ShapeDtypeStruct((t, n_heads * LANES), BF16),
        scratch_shapes=[
            pltpu.VMEM((hps, seq, LANES), F32),
            pltpu.VMEM((hps, 2, seq, LANES), BF16),
            pltpu.VMEM((hps, 2, n_chunks, LANES, LANES), F32),
            pltpu.VMEM((hps, 2, n_chunks, 8, LANES), F32),
            pltpu.VMEM((hps, n_chunks, LANES, 2 * LANES), BF16),
        ],
        compiler_params=pltpu.CompilerParams(
            dimension_semantics=("arbitrary", "arbitrary"),
            vmem_limit_bytes=VMEM_LIMIT_BYTES),
        name="hgrn2",
    )(*([proj] * 9), nw)


def _out_kernel(oa_ref, y_ref, zs_ref, x_ref, p_ref, wo_ref, wg_ref, wp_ref, cnw_ref, fw_ref,
                o_ref):
    d_a = oa_ref.shape[1]
    y = y_ref[...].astype(F32)
    ms_b = jnp.mean(y * y, axis=-1, keepdims=True)
    ob = (y * zs_ref[...].astype(F32) * (lax.rsqrt(ms_b + EPS) * cnw_ref[...])).astype(BF16)
    h1 = (x_ref[...]
          + jnp.dot(oa_ref[...], wo_ref[0:d_a, :], preferred_element_type=F32)
          + jnp.dot(ob, wo_ref[d_a:, :], preferred_element_type=F32))
    gate = _sigmoid(jnp.dot(h1.astype(BF16), wg_ref[...], preferred_element_type=F32))
    ple = jnp.dot(p_ref[...].astype(BF16), wp_ref[...], preferred_element_type=F32)
    h2 = h1 + ple * gate
    ms = jnp.mean(h2 * h2, axis=-1, keepdims=True)
    o_ref[...] = h2 * lax.rsqrt(ms + EPS) * fw_ref[...]


def _out(oa, y, zs, x2d, p2d, wo, wg, wp, cnw, fw, tm=256):
    t, d = x2d.shape

    def resident(shape):
        return pl.BlockSpec(shape, lambda i: (0, 0), pipeline_mode=pl.Buffered(1))

    def rows(a):
        return pl.BlockSpec((tm, a.shape[1]), lambda i: (i, 0))

    return pl.pallas_call(
        _out_kernel,
        grid=(t // tm,),
        in_specs=[rows(oa), rows(y), rows(zs), rows(x2d), rows(p2d),
                  resident(wo.shape), resident(wg.shape), resident(wp.shape),
                  resident((1, d)), resident((1, d))],
        out_specs=pl.BlockSpec((tm, d), lambda i: (i, 0)),
        out_shape=jax.ShapeDtypeStruct((t, d), F32),
        compiler_params=pltpu.CompilerParams(
            dimension_semantics=("arbitrary",),
            vmem_limit_bytes=VMEM_LIMIT_BYTES),
        name="outproj",
    )(oa, y, zs, x2d, p2d, wo, wg, wp, cnw.reshape(1, d), fw.reshape(1, d))


def kernel(x, p, norm_w, w_in, lb_theta, hgrn_norm_w, conv_w, conv_norm_w, w_out, w_ple,
           w_ple_gate, final_norm_w):
    batch, seq, d = x.shape
    depth = p.shape[0]
    assert depth == 1 and lb_theta.shape[1] == 2
    t = batch * seq
    n_heads = d // LANES

    x2d = x.reshape(t, d)
    hn = _rmsnorm(x2d, norm_w[0])
    w = w_in[0]

    theta2 = jnp.transpose(lb_theta, (1, 0, 2)).reshape(2, 2 * d)
    slabs, proj_a = _inproj_hgrn(hn, w, theta2, 2 * d, (0 * d, 4 * d), 1 * d, d)
    o_a = _hgrn(proj_a, slabs, hgrn_norm_w[0].reshape(n_heads, LANES), batch, seq, n_heads)

    y_b, zs_b, wo, wg, wp = _convproj(hn, w, conv_w[0], 5 * d, seq, d,
                                      (w_out[0], w_ple_gate[0], w_ple[0]))

    out = _out(o_a, y_b, zs_b, x2d, p[0].reshape(t, -1), wo, wg, wp,
               conv_norm_w[0], final_norm_w)
    return out.reshape(batch, seq, d)
```

```python
import jax
import jax.numpy as jnp
from jax import lax
from jax.experimental import pallas as pl
from jax.experimental.pallas import tpu as pltpu

EPS = 1e-6
LANES = 128
BF16_SUBLANES = 16
CHUNK = 64
HEADS_PER_STEP = 2
VMEM_CAPACITY_BYTES = 64 * 1024 * 1024
VMEM_LIMIT_BYTES = VMEM_CAPACITY_BYTES - 4 * 1024 * 1024

F32 = jnp.float32
BF16 = jnp.bfloat16


def _sigmoid(x):
    return 0.5 * jnp.tanh(0.5 * x) + 0.5


def _dot(a, b, dims=((1,), (0,))):
    return lax.dot_general(a, b, (dims, ((), ())), preferred_element_type=F32)


def _rmsnorm_kernel(x_ref, w_ref, o_ref):
    x = x_ref[...]
    ms = jnp.mean(x * x, axis=-1, keepdims=True)
    o_ref[...] = (x * lax.rsqrt(ms + EPS) * w_ref[...]).astype(o_ref.dtype)


def _rmsnorm(x2d, w, tm=512):
    t, d = x2d.shape
    return pl.pallas_call(
        _rmsnorm_kernel,
        grid=(t // tm,),
        in_specs=[pl.BlockSpec((tm, d), lambda i: (i, 0)),
                  pl.BlockSpec((1, d), lambda i: (0, 0))],
        out_specs=pl.BlockSpec((tm, d), lambda i: (i, 0)),
        out_shape=jax.ShapeDtypeStruct((t, d), BF16),
        name="rmsnorm",
    )(x2d, w.reshape(1, d))


def _store_heads(o_ref, val):
    for g in range(val.shape[1] // LANES):
        o_ref[g] = val[:, g * LANES:(g + 1) * LANES].astype(o_ref.dtype)


def _inproj_hgrn_kernel(h_ref, wg_ref, ws_ref, wv_ref, theta_ref, hi_ref, lo_ref, k_ref, s_ref,
                        v_ref, wbf_ref, acc_ref):
    tg = wg_ref.shape[1]
    ts = ws_ref.shape[1]

    @pl.when(pl.program_id(1) == 0)
    def _():
        wbf_ref[:, 0:tg] = wg_ref[...].astype(BF16)
        wbf_ref[:, tg:tg + ts] = ws_ref[...].astype(BF16)
        wbf_ref[:, tg + ts:] = wv_ref[...].astype(BF16)

    h = h_ref[...]
    acc_ref[...] = jnp.dot(h, wbf_ref[:, 0:tg], preferred_element_type=F32)
    t0 = theta_ref[0:1, :]
    t1 = theta_ref[1:2, :]
    m = jnp.maximum(t0, t1)
    e0 = jnp.exp(t0 - m)
    e1 = jnp.exp(t1 - m)
    lb = e0 / (e0 + e1)
    f = 0.5 * (1.0 + lb) + (0.5 * (1.0 - lb)) * jnp.tanh(0.5 * acc_ref[...])
    lf = jnp.log(f)
    hi = lf.astype(BF16)
    _store_heads(hi_ref, hi)
    _store_heads(lo_ref, lf - hi.astype(F32))
    _store_heads(k_ref, 1.0 - f)
    acc = jnp.dot(h, wbf_ref[:, tg:tg + ts], preferred_element_type=F32)
    _store_heads(s_ref, acc * _sigmoid(acc))
    _store_heads(v_ref, jnp.dot(h, wbf_ref[:, tg + ts:], preferred_element_type=F32))


def _inproj_hgrn(hn, w, theta2, gate_col_start, silu_col_starts, value_col_start, n_cols,
                 tm=1024, tg=512):
    t, d = hn.shape
    n_blocks = 2 * n_cols // tg
    half = n_blocks // 2
    tv = n_cols // n_blocks
    g0 = gate_col_start // tg
    s0, s1 = (cs // tg for cs in silu_col_starts)
    v0 = value_col_start // tv

    def out(width, n_slices):
        return (pl.BlockSpec((width // LANES, tm, LANES), lambda c, i: (c, i, 0)),
                jax.ShapeDtypeStruct((n_slices * n_cols // LANES, t, LANES), BF16))

    outs = [out(tg, 2)] * 4 + [out(tv, 1)]
    return pl.pallas_call(
        _inproj_hgrn_kernel,
        grid=(n_blocks, t // tm),
        in_specs=[pl.BlockSpec((tm, d), lambda c, i: (i, 0)),
                  pl.BlockSpec((d, tg), lambda c, i: (0, g0 + c)),
                  pl.BlockSpec((d, tg),
                               lambda c, i: (0, jnp.where(c < half, s0 + c, s1 + c - half))),
                  pl.BlockSpec((d, tv), lambda c, i: (0, v0 + c)),
                  pl.BlockSpec((theta2.shape[0], tg), lambda c, i: (0, c))],
        out_specs=[o[0] for o in outs],
        out_shape=[o[1] for o in outs],
        scratch_shapes=[pltpu.VMEM((d, 2 * tg + tv), BF16), pltpu.VMEM((tm, tg), F32)],
        compiler_params=pltpu.CompilerParams(
            dimension_semantics=("arbitrary", "arbitrary"),
            vmem_limit_bytes=VMEM_LIMIT_BYTES),
        name="inproj_hgrn",
    )(hn, w, w, w, theta2)


def _convproj_kernel(h_ref, wb_ref, wc_ref, wh_ref, wz_ref, cw_ref, wo_ref, wg_ref, wp_ref,
                     y_ref, zs_ref, wo_bf_ref, wg_bf_ref, wp_bf_ref, wbf_ref):
    seq = h_ref.shape[0]
    wo_bf_ref[...] = wo_ref[...].astype(BF16)
    wg_bf_ref[...] = wg_ref[...].astype(BF16)
    wp_bf_ref[...] = wp_ref[...].astype(BF16)

    @pl.when(pl.program_id(1) == 0)
    def _():
        for k, w_ref in enumerate((wb_ref, wc_ref, wh_ref, wz_ref)):
            wbf_ref[k] = w_ref[...].astype(BF16)

    h = h_ref[...]
    cg = jnp.dot(h, wbf_ref[1], preferred_element_type=F32)
    hb = jnp.dot(h, wbf_ref[2], preferred_element_type=F32)
    u = cg * hb
    row = lax.broadcasted_iota(jnp.int32, u.shape, 0)
    u_prev = jnp.where(row == 0, 0.0, pltpu.roll(u, 1, axis=0))
    u_next = jnp.where(row == seq - 1, 0.0, pltpu.roll(u, seq - 1, axis=0))
    conv = cw_ref[0:1, :] * u_prev + cw_ref[1:2, :] * u + cw_ref[2:3, :] * u_next
    zb = jnp.dot(h, wbf_ref[3], preferred_element_type=F32)
    zs_ref[...] = (zb * _sigmoid(zb)).astype(zs_ref.dtype)
    bg = jnp.dot(h, wbf_ref[0], preferred_element_type=F32)
    y_ref[...] = (bg * conv).astype(y_ref.dtype)


def _convproj(hn, w, cw, col_start, seq, n_ch, out_weights, tnc=256):
    t, d = hn.shape
    batch = t // seq
    blk0 = col_start // tnc
    per_slice = n_ch // tnc
    n_steps = per_slice * batch

    def w_spec(k):
        return pl.BlockSpec((d, tnc), lambda c, b, k=k: (0, blk0 + k * per_slice + c))

    def slab_spec(a):
        rows = max(a.shape[0] // n_steps, BF16_SUBLANES)
        last = a.shape[0] // rows - 1
        return pl.BlockSpec((rows, a.shape[1]),
                            lambda c, b: (jnp.minimum(c * batch + b, last), 0))

    slabs = [slab_spec(a) for a in out_weights]
    out_spec = pl.BlockSpec((seq, tnc), lambda c, b: (b, c))
    out_sds = jax.ShapeDtypeStruct((t, n_ch), BF16)
    return pl.pallas_call(
        _convproj_kernel,
        grid=(per_slice, batch),
        in_specs=[pl.BlockSpec((seq, d), lambda c, b: (b, 0)),
                  w_spec(0), w_spec(1), w_spec(2), w_spec(3),
                  pl.BlockSpec((cw.shape[0], tnc), lambda c, b: (0, c))] + slabs,
        out_specs=[out_spec, out_spec] + slabs,
        out_shape=[out_sds, out_sds] + [jax.ShapeDtypeStruct(a.shape, BF16)
                                        for a in out_weights],
        scratch_shapes=[pltpu.VMEM((4, d, tnc), BF16)],
        compiler_params=pltpu.CompilerParams(
            dimension_semantics=("arbitrary", "arbitrary"),
            vmem_limit_bytes=VMEM_LIMIT_BYTES),
        name="convproj",
    )(hn, w, w, w, w, cw, *out_weights)


def _hgrn_kernel(qs_ref, v_ref, hif_ref, lof_ref, kf_ref, hib_ref, lob_ref, kb_ref, zs_ref,
                 nw_ref, o_ref, acc_ref, qin_ref, kv_ref, dec_ref, sprev_ref):
    seq = qs_ref.shape[1]
    c = CHUNK
    n_chunks = seq // c
    head0 = pl.program_id(1) * HEADS_PER_STEP
    mid = c // 2 - 1

    row = lax.broadcasted_iota(jnp.int32, (c, c), 0)
    col = lax.broadcasted_iota(jnp.int32, (c, c), 1)
    masks = (col <= row, col >= row)
    row2 = lax.broadcasted_iota(jnp.int32, (c, 2 * c), 0)
    col2 = lax.broadcasted_iota(jnp.int32, (c, 2 * c), 1) % c
    tris = ((col2 <= row2).astype(BF16), (col2 >= row2).astype(BF16))
    hi_refs = (hif_ref, hib_ref)
    lo_refs = (lof_ref, lob_ref)
    k_refs = (kf_ref, kb_ref)
    mid_rows = (mid, c - 1 - mid)
    last_rows = (c - 1, 0)

    def rows(j):
        return slice(j * c, (j + 1) * c)

    def intra(hh):
        qs = qs_ref[hh].astype(F32)
        v = v_ref[hh]
        ks, bs = [], []
        for d in range(2):
            hi = hi_refs[d][hh]
            lo = lo_refs[d][hh]
            ks.append(k_refs[d][hh].astype(F32))
            bs.append([_dot(tris[d], jnp.concatenate([hi[rows(j)], lo[rows(j)]], axis=0))
                       for j in range(n_chunks)])
        ss = ([], [])
        for j in range(n_chunks):
            for d in range(2):
                b = bs[d][j]
                b_mid = b[mid_rows[d]:mid_rows[d] + 1, :]
                b_last = b[last_rows[d]:last_rows[d] + 1, :]
                q_rel = qs[rows(j)] * jnp.exp(b - b_mid)
                k_rel = ks[d][rows(j)] * jnp.exp(b_mid - b)
                ss[d].append(_dot(q_rel.astype(BF16), k_rel.astype(BF16), ((1,), (1,))))
                q_in = q_rel * jnp.exp(b_mid)
                k_st = k_rel * jnp.exp(b_last - b_mid)
                qin_ref[hh, d, rows(j), :] = q_in.astype(BF16)
                kv_ref[hh, d, j] = _dot(v[rows(j)], k_st.astype(BF16), ((0,), (0,)))
                dec_ref[hh, d, j] = jnp.broadcast_to(jnp.exp(b_last), (8, LANES))
        for j in range(n_chunks):
            p = jnp.where(masks[0], ss[0][j], 0.0) + jnp.where(masks[1], ss[1][j], 0.0)
            acc_ref[hh, rows(j), :] = _dot(p.astype(BF16), v[rows(j)])

    def scan(hh):
        st_f = jnp.zeros((LANES, LANES), F32)
        st_b = st_f
        for i in range(n_chunks):
            nb = n_chunks - 1 - i
            sprev_ref[hh, i, :, 0:LANES] = st_f.astype(BF16)
            sprev_ref[hh, nb, :, LANES:2 * LANES] = st_b.astype(BF16)
            st_f = st_f * dec_ref[hh, 0, i][0:1, :] + kv_ref[hh, 0, i]
            st_b = st_b * dec_ref[hh, 1, nb][0:1, :] + kv_ref[hh, 1, nb]

    def finish(hh):
        inter = []
        for j in range(n_chunks):
            qin = jnp.concatenate([qin_ref[hh, 0, rows(j), :], qin_ref[hh, 1, rows(j), :]],
                                  axis=1)
            inter.append(_dot(qin, sprev_ref[hh, j], ((1,), (1,))))
        o = acc_ref[hh] + jnp.concatenate(inter, axis=0)
        ms = jnp.mean(o * o, axis=-1, keepdims=True)
        nw = nw_ref[pl.ds(head0 + hh, 1), :]
        o_ref[:, hh * LANES:(hh + 1) * LANES] = (
            o * lax.rsqrt(ms + EPS) * nw * zs_ref[hh].astype(F32)).astype(o_ref.dtype)

    for stage in (intra, scan, finish):
        for hh in range(HEADS_PER_STEP):
            stage(hh)


def _hgrn(sil, v, hi, lo, k, nw, batch, seq, n_heads):
    t = batch * seq
    n_chunks = seq // CHUNK
    hps = HEADS_PER_STEP
    groups = n_heads // hps

    def spec(second_half):
        return pl.BlockSpec((hps, seq, LANES), lambda b, g: (second_half * groups + g, b, 0))

    fwd, bwd = spec(0), spec(1)
    return pl.pallas_call(
        _hgrn_kernel,
        grid=(batch, n_heads // hps),
        in_specs=[fwd, fwd, fwd, fwd, fwd, bwd, bwd, bwd, bwd,
                  pl.BlockSpec(nw.shape, lambda b, g: (0, 0))],
        out_specs=pl.BlockSpec((seq, hps * LANES), lambda b, g: (b, g)),
        out_shape=jax.ShapeDtypeStruct((t, n_heads * LANES), BF16),
        scratch_shapes=[
            pltpu.VMEM((hps, seq, LANES), F32),
            pltpu.VMEM((hps, 2, seq, LANES), BF16),
            pltpu.VMEM((hps, 2, n_chunks, LANES, LANES), F32),
            pltpu.VMEM((hps, 2, n_chunks, 8, LANES), F32),
            pltpu.VMEM((hps, n_chunks, LANES, 2 * LANES), BF16),
        ],
        compiler_params=pltpu.CompilerParams(
            dimension_semantics=("arbitrary", "arbitrary"),
            vmem_limit_bytes=VMEM_LIMIT_BYTES),
        name="hgrn2",
    )(sil, v, hi, lo, k, hi, lo, k, sil, nw)


def _out_kernel(oa_ref, y_ref, zs_ref, x_ref, p_ref, wo_ref, wg_ref, wp_ref, cnw_ref, fw_ref,
                o_ref):
    d_a = oa_ref.shape[1]
    y = y_ref[...].astype(F32)
    ms_b = jnp.mean(y * y, axis=-1, keepdims=True)
    ob = (y * zs_ref[...].astype(F32) * (lax.rsqrt(ms_b + EPS) * cnw_ref[...])).astype(BF16)
    h1 = (x_ref[...]
          + jnp.dot(oa_ref[...], wo_ref[0:d_a, :], preferred_element_type=F32)
          + jnp.dot(ob, wo_ref[d_a:, :], preferred_element_type=F32))
    gate = _sigmoid(jnp.dot(h1.astype(BF16), wg_ref[...], preferred_element_type=F32))
    ple = jnp.dot(p_ref[...].astype(BF16), wp_ref[...], preferred_element_type=F32)
    h2 = h1 + ple * gate
    ms = jnp.mean(h2 * h2, axis=-1, keepdims=True)
    o_ref[...] = h2 * lax.rsqrt(ms + EPS) * fw_ref[...]


def _out(oa, y, zs, x2d, p2d, wo, wg, wp, cnw, fw, tm=256):
    t, d = x2d.shape

    def resident(shape):
        return pl.BlockSpec(shape, lambda i: (0, 0), pipeline_mode=pl.Buffered(1))

    def rows(a):
        return pl.BlockSpec((tm, a.shape[1]), lambda i: (i, 0))

    return pl.pallas_call(
        _out_kernel,
        grid=(t // tm,),
        in_specs=[rows(oa), rows(y), rows(zs), rows(x2d), rows(p2d),
                  resident(wo.shape), resident(wg.shape), resident(wp.shape),
                  resident((1, d)), resident((1, d))],
        out_specs=pl.BlockSpec((tm, d), lambda i: (i, 0)),
        out_shape=jax.ShapeDtypeStruct((t, d), F32),
        compiler_params=pltpu.CompilerParams(
            dimension_semantics=("arbitrary",),
            vmem_limit_bytes=VMEM_LIMIT_BYTES),
        name="outproj",
    )(oa, y, zs, x2d, p2d, wo, wg, wp, cnw.reshape(1, d), fw.reshape(1, d))


def kernel(x, p, norm_w, w_in, lb_theta, hgrn_norm_w, conv_w, conv_norm_w, w_out, w_ple,
           w_ple_gate, final_norm_w):
    batch, seq, d = x.shape
    depth = p.shape[0]
    assert depth == 1 and lb_theta.shape[1] == 2
    t = batch * seq
    n_heads = d // LANES

    x2d = x.reshape(t, d)
    hn = _rmsnorm(x2d, norm_w[0])
    w = w_in[0]

    theta2 = jnp.transpose(lb_theta, (1, 0, 2)).reshape(2, 2 * d)
    hi, lo, k, sil, v = _inproj_hgrn(hn, w, theta2, 2 * d, (0 * d, 4 * d), 1 * d, d)
    o_a = _hgrn(sil, v, hi, lo, k, hgrn_norm_w[0].reshape(n_heads, LANES), batch, seq, n_heads)

    y_b, zs_b, wo, wg, wp = _convproj(hn, w, conv_w[0], 5 * d, seq, d,
                                      (w_out[0], w_ple_gate[0], w_ple[0]))

    out = _out(o_a, y_b, zs_b, x2d, p[0].reshape(t, -1), wo, wg, wp,
               conv_norm_w[0], final_norm_w)
    return out.reshape(batch, seq, d)
```

```python
import jax
import jax.numpy as jnp
from jax import lax
from jax.experimental import pallas as pl
from jax.experimental.pallas import tpu as pltpu

EPS = 1e-6
LANES = 128
BF16_SUBLANES = 16
CHUNK = 64
HEADS_PER_STEP = 2
VMEM_CAPACITY_BYTES = 64 * 1024 * 1024
VMEM_LIMIT_BYTES = VMEM_CAPACITY_BYTES - 4 * 1024 * 1024

F32 = jnp.float32
BF16 = jnp.bfloat16


def _sigmoid(x):
    return 0.5 * jnp.tanh(0.5 * x) + 0.5


def _dot(a, b, dims=((1,), (0,))):
    return lax.dot_general(a, b, (dims, ((), ())), preferred_element_type=F32)


def _rmsnorm_kernel(x_ref, w_ref, o_ref):
    x = x_ref[...]
    ms = jnp.mean(x * x, axis=-1, keepdims=True)
    o_ref[...] = (x * lax.rsqrt(ms + EPS) * w_ref[...]).astype(o_ref.dtype)


def _rmsnorm(x2d, w, tm=1024):
    t, d = x2d.shape
    return pl.pallas_call(
        _rmsnorm_kernel,
        grid=(t // tm,),
        in_specs=[pl.BlockSpec((tm, d), lambda i: (i, 0)),
                  pl.BlockSpec((1, d), lambda i: (0, 0))],
        out_specs=pl.BlockSpec((tm, d), lambda i: (i, 0)),
        out_shape=jax.ShapeDtypeStruct((t, d), BF16),
        name="rmsnorm",
    )(x2d, w.reshape(1, d))


def _store_heads(o_ref, val):
    for g in range(val.shape[1] // LANES):
        o_ref[g] = val[:, g * LANES:(g + 1) * LANES].astype(o_ref.dtype)


def _inproj_hgrn_kernel(h_ref, wg_ref, ws_ref, wv_ref, theta_ref, hi_ref, lo_ref, k_ref, s_ref,
                        v_ref, wbf_ref):
    tg = wg_ref.shape[1]
    ts = ws_ref.shape[1]

    @pl.when(pl.program_id(1) == 0)
    def _():
        wbf_ref[:, 0:tg] = wg_ref[...].astype(BF16)
        wbf_ref[:, tg:tg + ts] = ws_ref[...].astype(BF16)
        wbf_ref[:, tg + ts:] = wv_ref[...].astype(BF16)

    h = h_ref[...]
    acc = jnp.dot(h, wbf_ref[:, 0:tg], preferred_element_type=F32)
    t0 = theta_ref[0:1, :]
    t1 = theta_ref[1:2, :]
    m = jnp.maximum(t0, t1)
    e0 = jnp.exp(t0 - m)
    e1 = jnp.exp(t1 - m)
    lb = e0 / (e0 + e1)
    f = 0.5 * (1.0 + lb) + (0.5 * (1.0 - lb)) * jnp.tanh(0.5 * acc)
    lf = jnp.log(f)
    hi = lf.astype(BF16)
    _store_heads(hi_ref, hi)
    _store_heads(lo_ref, lf - hi.astype(F32))
    _store_heads(k_ref, 1.0 - f)
    acc = jnp.dot(h, wbf_ref[:, tg:tg + ts], preferred_element_type=F32)
    _store_heads(s_ref, acc * _sigmoid(acc))
    _store_heads(v_ref, jnp.dot(h, wbf_ref[:, tg + ts:], preferred_element_type=F32))


def _inproj_hgrn(hn, w, theta2, gate_col_start, silu_col_starts, value_col_start, n_cols,
                 tm=1024, tg=512):
    t, d = hn.shape
    n_blocks = 2 * n_cols // tg
    half = n_blocks // 2
    tv = n_cols // n_blocks
    g0 = gate_col_start // tg
    s0, s1 = (cs // tg for cs in silu_col_starts)
    v0 = value_col_start // tv

    def out(width, n_slices):
        return (pl.BlockSpec((width // LANES, tm, LANES), lambda c, i: (c, i, 0)),
                jax.ShapeDtypeStruct((n_slices * n_cols // LANES, t, LANES), BF16))

    outs = [out(tg, 2)] * 4 + [out(tv, 1)]
    return pl.pallas_call(
        _inproj_hgrn_kernel,
        grid=(n_blocks, t // tm),
        in_specs=[pl.BlockSpec((tm, d), lambda c, i: (i, 0)),
                  pl.BlockSpec((d, tg), lambda c, i: (0, g0 + c)),
                  pl.BlockSpec((d, tg),
                               lambda c, i: (0, jnp.where(c < half, s0 + c, s1 + c - half))),
                  pl.BlockSpec((d, tv), lambda c, i: (0, v0 + c)),
                  pl.BlockSpec((theta2.shape[0], tg), lambda c, i: (0, c))],
        out_specs=[o[0] for o in outs],
        out_shape=[o[1] for o in outs],
        scratch_shapes=[pltpu.VMEM((d, 2 * tg + tv), BF16)],
        compiler_params=pltpu.CompilerParams(
            dimension_semantics=("arbitrary", "arbitrary"),
            vmem_limit_bytes=VMEM_LIMIT_BYTES),
        name="inproj_hgrn",
    )(hn, w, w, w, theta2)


def _convproj_kernel(h_ref, wb_ref, wc_ref, wh_ref, wz_ref, cw_ref, wo_ref, wg_ref, wp_ref,
                     y_ref, zs_ref, wo_bf_ref, wg_bf_ref, wp_bf_ref, wbf_ref):
    seq = h_ref.shape[0]
    wo_bf_ref[...] = wo_ref[...].astype(BF16)
    wg_bf_ref[...] = wg_ref[...].astype(BF16)
    wp_bf_ref[...] = wp_ref[...].astype(BF16)

    @pl.when(pl.program_id(1) == 0)
    def _():
        for k, w_ref in enumerate((wb_ref, wc_ref, wh_ref, wz_ref)):
            wbf_ref[k] = w_ref[...].astype(BF16)

    h = h_ref[...]
    cg = jnp.dot(h, wbf_ref[1], preferred_element_type=F32)
    hb = jnp.dot(h, wbf_ref[2], preferred_element_type=F32)
    u = cg * hb
    row = lax.broadcasted_iota(jnp.int32, u.shape, 0)
    u_prev = jnp.where(row == 0, 0.0, pltpu.roll(u, 1, axis=0))
    u_next = jnp.where(row == seq - 1, 0.0, pltpu.roll(u, seq - 1, axis=0))
    conv = cw_ref[0:1, :] * u_prev + cw_ref[1:2, :] * u + cw_ref[2:3, :] * u_next
    zb = jnp.dot(h, wbf_ref[3], preferred_element_type=F32)
    zs_ref[...] = (zb * _sigmoid(zb)).astype(zs_ref.dtype)
    bg = jnp.dot(h, wbf_ref[0], preferred_element_type=F32)
    y_ref[...] = (bg * conv).astype(y_ref.dtype)


def _convproj(hn, w, cw, col_start, seq, n_ch, out_weights, tnc=256):
    t, d = hn.shape
    batch = t // seq
    blk0 = col_start // tnc
    per_slice = n_ch // tnc
    n_steps = per_slice * batch

    def w_spec(k):
        return pl.BlockSpec((d, tnc), lambda c, b, k=k: (0, blk0 + k * per_slice + c))

    def slab_spec(a):
        rows = max(a.shape[0] // n_steps, BF16_SUBLANES)
        last = a.shape[0] // rows - 1
        return pl.BlockSpec((rows, a.shape[1]),
                            lambda c, b: (jnp.minimum(c * batch + b, last), 0))

    slabs = [slab_spec(a) for a in out_weights]
    out_spec = pl.BlockSpec((seq, tnc), lambda c, b: (b, c))
    out_sds = jax.ShapeDtypeStruct((t, n_ch), BF16)
    return pl.pallas_call(
        _convproj_kernel,
        grid=(per_slice, batch),
        in_specs=[pl.BlockSpec((seq, d), lambda c, b: (b, 0)),
                  w_spec(0), w_spec(1), w_spec(2), w_spec(3),
                  pl.BlockSpec((cw.shape[0], tnc), lambda c, b: (0, c))] + slabs,
        out_specs=[out_spec, out_spec] + slabs,
        out_shape=[out_sds, out_sds] + [jax.ShapeDtypeStruct(a.shape, BF16)
                                        for a in out_weights],
        scratch_shapes=[pltpu.VMEM((4, d, tnc), BF16)],
        compiler_params=pltpu.CompilerParams(
            dimension_semantics=("arbitrary", "arbitrary"),
            vmem_limit_bytes=VMEM_LIMIT_BYTES),
        name="convproj",
    )(hn, w, w, w, w, cw, *out_weights)


def _hgrn_kernel(qs_ref, v_ref, hif_ref, lof_ref, kf_ref, hib_ref, lob_ref, kb_ref, zs_ref,
                 nw_ref, o_ref, acc_ref, qin_ref, kv_ref, dec_ref, sprev_ref):
    seq = qs_ref.shape[1]
    c = CHUNK
    n_chunks = seq // c
    head0 = pl.program_id(1) * HEADS_PER_STEP
    mid = c // 2 - 1

    row = lax.broadcasted_iota(jnp.int32, (c, c), 0)
    col = lax.broadcasted_iota(jnp.int32, (c, c), 1)
    masks = (col <= row, col >= row)
    row2 = lax.broadcasted_iota(jnp.int32, (c, 2 * c), 0)
    col2 = lax.broadcasted_iota(jnp.int32, (c, 2 * c), 1) % c
    tris = ((col2 <= row2).astype(BF16), (col2 >= row2).astype(BF16))
    hi_refs = (hif_ref, hib_ref)
    lo_refs = (lof_ref, lob_ref)
    k_refs = (kf_ref, kb_ref)
    mid_rows = (mid, c - 1 - mid)
    last_rows = (c - 1, 0)

    def rows(j):
        return slice(j * c, (j + 1) * c)

    def intra(hh):
        qs = qs_ref[hh].astype(F32)
        v = v_ref[hh]
        ks, bs = [], []
        for d in range(2):
            hi = hi_refs[d][hh]
            lo = lo_refs[d][hh]
            ks.append(k_refs[d][hh].astype(F32))
            bs.append([_dot(tris[d], jnp.concatenate([hi[rows(j)], lo[rows(j)]], axis=0))
                       for j in range(n_chunks)])
        ss = ([], [])
        for j in range(n_chunks):
            for d in range(2):
                b = bs[d][j]
                b_mid = b[mid_rows[d]:mid_rows[d] + 1, :]
                b_last = b[last_rows[d]:last_rows[d] + 1, :]
                q_rel = qs[rows(j)] * jnp.exp(b - b_mid)
                k_rel = ks[d][rows(j)] * jnp.exp(b_mid - b)
                ss[d].append(_dot(q_rel.astype(BF16), k_rel.astype(BF16), ((1,), (1,))))
                q_in = q_rel * jnp.exp(b_mid)
                k_st = k_rel * jnp.exp(b_last - b_mid)
                qin_ref[hh, d, rows(j), :] = q_in.astype(BF16)
                kv_ref[hh, d, j] = _dot(v[rows(j)], k_st.astype(BF16), ((0,), (0,)))
                dec_ref[hh, d, j] = jnp.broadcast_to(jnp.exp(b_last), (8, LANES))
        for j in range(n_chunks):
            p = jnp.where(masks[0], ss[0][j], 0.0) + jnp.where(masks[1], ss[1][j], 0.0)
            acc_ref[hh, rows(j), :] = _dot(p.astype(BF16), v[rows(j)])

    def scan(hh):
        st_f = jnp.zeros((LANES, LANES), F32)
        st_b = st_f
        for i in range(n_chunks):
            nb = n_chunks - 1 - i
            sprev_ref[hh, i, :, 0:LANES] = st_f.astype(BF16)
            sprev_ref[hh, nb, :, LANES:2 * LANES] = st_b.astype(BF16)
            st_f = st_f * dec_ref[hh, 0, i][0:1, :] + kv_ref[hh, 0, i]
            st_b = st_b * dec_ref[hh, 1, nb][0:1, :] + kv_ref[hh, 1, nb]

    def finish(hh):
        inter = []
        for j in range(n_chunks):
            qin = jnp.concatenate([qin_ref[hh, 0, rows(j), :], qin_ref[hh, 1, rows(j), :]],
                                  axis=1)
            inter.append(_dot(qin, sprev_ref[hh, j], ((1,), (1,))))
        o = acc_ref[hh] + jnp.concatenate(inter, axis=0)
        ms = jnp.mean(o * o, axis=-1, keepdims=True)
        nw = nw_ref[pl.ds(head0 + hh, 1), :]
        o_ref[:, hh * LANES:(hh + 1) * LANES] = (
            o * lax.rsqrt(ms + EPS) * nw * zs_ref[hh].astype(F32)).astype(o_ref.dtype)

    for stage in (intra, scan, finish):
        for hh in range(HEADS_PER_STEP):
            stage(hh)


def _hgrn(sil, v, hi, lo, k, nw, batch, seq, n_heads):
    t = batch * seq
    n_chunks = seq // CHUNK
    hps = HEADS_PER_STEP
    groups = n_heads // hps

    def spec(second_half):
        return pl.BlockSpec((hps, seq, LANES), lambda b, g: (second_half * groups + g, b, 0))

    fwd, bwd = spec(0), spec(1)
    return pl.pallas_call(
        _hgrn_kernel,
        grid=(batch, n_heads // hps),
        in_specs=[fwd, fwd, fwd, fwd, fwd, bwd, bwd, bwd, bwd,
                  pl.BlockSpec(nw.shape, lambda b, g: (0, 0))],
        out_specs=pl.BlockSpec((seq, hps * LANES), lambda b, g: (b, g)),
        out_shape=jax.ShapeDtypeStruct((t, n_heads * LANES), BF16),
        scratch_shapes=[
            pltpu.VMEM((hps, seq, LANES), F32),
            pltpu.VMEM((hps, 2, seq, LANES), BF16),
            pltpu.VMEM((hps, 2, n_chunks, LANES, LANES), F32),
            pltpu.VMEM((hps, 2, n_chunks, 8, LANES), F32),
            pltpu.VMEM((hps, n_chunks, LANES, 2 * LANES), BF16),
        ],
        compiler_params=pltpu.CompilerParams(
            dimension_semantics=("arbitrary", "arbitrary"),
            vmem_limit_bytes=VMEM_LIMIT_BYTES),
        name="hgrn2",
    )(sil, v, hi, lo, k, hi, lo, k, sil, nw)


def _out_kernel(oa_ref, y_ref, zs_ref, x_ref, p_ref, wo_ref, wg_ref, wp_ref, cnw_ref, fw_ref,
                o_ref):
    d_a = oa_ref.shape[1]
    y = y_ref[...].astype(F32)
    ms_b = jnp.mean(y * y, axis=-1, keepdims=True)
    ob = (y * zs_ref[...].astype(F32) * (lax.rsqrt(ms_b + EPS) * cnw_ref[...])).astype(BF16)
    h1 = (x_ref[...]
          + jnp.dot(oa_ref[...], wo_ref[0:d_a, :], preferred_element_type=F32)
          + jnp.dot(ob, wo_ref[d_a:, :], preferred_element_type=F32))
    gate = _sigmoid(jnp.dot(h1.astype(BF16), wg_ref[...], preferred_element_type=F32))
    ple = jnp.dot(p_ref[...].astype(BF16), wp_ref[...], preferred_element_type=F32)
    h2 = h1 + ple * gate
    ms = jnp.mean(h2 * h2, axis=-1, keepdims=True)
    o_ref[...] = h2 * lax.rsqrt(ms + EPS) * fw_ref[...]


def _out(oa, y, zs, x2d, p2d, wo, wg, wp, cnw, fw, tm=256):
    t, d = x2d.shape

    def resident(shape):
        return pl.BlockSpec(shape, lambda i: (0, 0), pipeline_mode=pl.Buffered(1))

    def rows(a):
        return pl.BlockSpec((tm, a.shape[1]), lambda i: (i, 0))

    return pl.pallas_call(
        _out_kernel,
        grid=(t // tm,),
        in_specs=[rows(oa), rows(y), rows(zs), rows(x2d), rows(p2d),
                  resident(wo.shape), resident(wg.shape), resident(wp.shape),
                  resident((1, d)), resident((1, d))],
        out_specs=pl.BlockSpec((tm, d), lambda i: (i, 0)),
        out_shape=jax.ShapeDtypeStruct((t, d), F32),
        compiler_params=pltpu.CompilerParams(
            dimension_semantics=("arbitrary",),
            vmem_limit_bytes=VMEM_LIMIT_BYTES),
        name="outproj",
    )(oa, y, zs, x2d, p2d, wo, wg, wp, cnw.reshape(1, d), fw.reshape(1, d))


def kernel(x, p, norm_w, w_in, lb_theta, hgrn_norm_w, conv_w, conv_norm_w, w_out, w_ple,
           w_ple_gate, final_norm_w):
    batch, seq, d = x.shape
    depth = p.shape[0]
    assert depth == 1 and lb_theta.shape[1] == 2
    t = batch * seq
    n_heads = d // LANES

    x2d = x.reshape(t, d)
    hn = _rmsnorm(x2d, norm_w[0])
    w = w_in[0]

    theta2 = jnp.transpose(lb_theta, (1, 0, 2)).reshape(2, 2 * d)
    hi, lo, k, sil, v = _inproj_hgrn(hn, w, theta2, 2 * d, (0 * d, 4 * d), 1 * d, d)
    o_a = _hgrn(sil, v, hi, lo, k, hgrn_norm_w[0].reshape(n_heads, LANES), batch, seq, n_heads)

    y_b, zs_b, wo, wg, wp = _convproj(hn, w, conv_w[0], 5 * d, seq, d,
                                      (w_out[0], w_ple_gate[0], w_ple[0]))

    out = _out(o_a, y_b, zs_b, x2d, p[0].reshape(t, -1), wo, wg, wp,
               conv_norm_w[0], final_norm_w)
    return out.reshape(batch, seq, d)
```

```python
import jax
import jax.numpy as jnp
from jax import lax
from jax.experimental import pallas as pl
from jax.experimental.pallas import tpu as pltpu

EPS = 1e-6
LANES = 128
BF16_SUBLANES = 16
CHUNK = 64
HEADS_PER_STEP = 2
VMEM_CAPACITY_BYTES = 64 * 1024 * 1024
VMEM_LIMIT_BYTES = VMEM_CAPACITY_BYTES - 4 * 1024 * 1024

F32 = jnp.float32
BF16 = jnp.bfloat16


def _sigmoid(x):
    return 0.5 * jnp.tanh(0.5 * x) + 0.5


def _dot(a, b, dims=((1,), (0,))):
    return lax.dot_general(a, b, (dims, ((), ())), preferred_element_type=F32)


def _rmsnorm_kernel(x_ref, w_ref, o_ref):
    x = x_ref[...]
    ms = jnp.mean(x * x, axis=-1, keepdims=True)
    o_ref[...] = (x * lax.rsqrt(ms + EPS) * w_ref[...]).astype(o_ref.dtype)


def _rmsnorm(x2d, w, tm=1024):
    t, d = x2d.shape
    return pl.pallas_call(
        _rmsnorm_kernel,
        grid=(t // tm,),
        in_specs=[pl.BlockSpec((tm, d), lambda i: (i, 0)),
                  pl.BlockSpec((1, d), lambda i: (0, 0))],
        out_specs=pl.BlockSpec((tm, d), lambda i: (i, 0)),
        out_shape=jax.ShapeDtypeStruct((t, d), BF16),
        name="rmsnorm",
    )(x2d, w.reshape(1, d))


def _store_heads(o_ref, val):
    for g in range(val.shape[1] // LANES):
        o_ref[g] = val[:, g * LANES:(g + 1) * LANES].astype(o_ref.dtype)


def _inproj_hgrn_kernel(h_ref, wg_ref, ws_ref, wv_ref, theta_ref, hi_ref, lo_ref, k_ref, s_ref,
                        v_ref, wbf_ref, acc_ref):
    tg = wg_ref.shape[1]
    ts = ws_ref.shape[1]

    @pl.when(pl.program_id(1) == 0)
    def _():
        wbf_ref[:, 0:tg] = wg_ref[...].astype(BF16)
        wbf_ref[:, tg:tg + ts] = ws_ref[...].astype(BF16)
        wbf_ref[:, tg + ts:] = wv_ref[...].astype(BF16)

    h = h_ref[...]
    acc_ref[...] = jnp.dot(h, wbf_ref[:, 0:tg], preferred_element_type=F32)
    t0 = theta_ref[0:1, :]
    t1 = theta_ref[1:2, :]
    m = jnp.maximum(t0, t1)
    e0 = jnp.exp(t0 - m)
    e1 = jnp.exp(t1 - m)
    lb = e0 / (e0 + e1)
    f = 0.5 * (1.0 + lb) + (0.5 * (1.0 - lb)) * jnp.tanh(0.5 * acc_ref[...])
    lf = jnp.log(f)
    hi = lf.astype(BF16)
    _store_heads(hi_ref, hi)
    _store_heads(lo_ref, lf - hi.astype(F32))
    _store_heads(k_ref, 1.0 - f)
    acc = jnp.dot(h, wbf_ref[:, tg:tg + ts], preferred_element_type=F32)
    _store_heads(s_ref, acc * _sigmoid(acc))
    _store_heads(v_ref, jnp.dot(h, wbf_ref[:, tg + ts:], preferred_element_type=F32))


def _inproj_hgrn(hn, w, theta2, gate_col_start, silu_col_starts, value_col_start, n_cols,
                 tm=1024, tg=512):
    t, d = hn.shape
    n_blocks = 2 * n_cols // tg
    half = n_blocks // 2
    tv = n_cols // n_blocks
    g0 = gate_col_start // tg
    s0, s1 = (cs // tg for cs in silu_col_starts)
    v0 = value_col_start // tv

    def out(width, n_slices):
        return (pl.BlockSpec((width // LANES, tm, LANES), lambda c, i: (c, i, 0)),
                jax.ShapeDtypeStruct((n_slices * n_cols // LANES, t, LANES), BF16))

    outs = [out(tg, 2)] * 4 + [out(tv, 1)]
    return pl.pallas_call(
        _inproj_hgrn_kernel,
        grid=(n_blocks, t // tm),
        in_specs=[pl.BlockSpec((tm, d), lambda c, i: (i, 0)),
                  pl.BlockSpec((d, tg), lambda c, i: (0, g0 + c)),
                  pl.BlockSpec((d, tg),
                               lambda c, i: (0, jnp.where(c < half, s0 + c, s1 + c - half))),
                  pl.BlockSpec((d, tv), lambda c, i: (0, v0 + c)),
                  pl.BlockSpec((theta2.shape[0], tg), lambda c, i: (0, c))],
        out_specs=[o[0] for o in outs],
        out_shape=[o[1] for o in outs],
        scratch_shapes=[pltpu.VMEM((d, 2 * tg + tv), BF16), pltpu.VMEM((tm, tg), F32)],
        compiler_params=pltpu.CompilerParams(
            dimension_semantics=("arbitrary", "arbitrary"),
            vmem_limit_bytes=VMEM_LIMIT_BYTES),
        name="inproj_hgrn",
    )(hn, w, w, w, theta2)


def _convproj_kernel(h_ref, wb_ref, wc_ref, wh_ref, wz_ref, cw_ref, wo_ref, wg_ref, wp_ref,
                     y_ref, zs_ref, wo_bf_ref, wg_bf_ref, wp_bf_ref, wbf_ref):
    seq = h_ref.shape[0]
    wo_bf_ref[...] = wo_ref[...].astype(BF16)
    wg_bf_ref[...] = wg_ref[...].astype(BF16)
    wp_bf_ref[...] = wp_ref[...].astype(BF16)

    @pl.when(pl.program_id(1) == 0)
    def _():
        for k, w_ref in enumerate((wb_ref, wc_ref, wh_ref, wz_ref)):
            wbf_ref[k] = w_ref[...].astype(BF16)

    h = h_ref[...]
    cg = jnp.dot(h, wbf_ref[1], preferred_element_type=F32)
    hb = jnp.dot(h, wbf_ref[2], preferred_element_type=F32)
    u = cg * hb
    row = lax.broadcasted_iota(jnp.int32, u.shape, 0)
    u_prev = jnp.where(row == 0, 0.0, pltpu.roll(u, 1, axis=0))
    u_next = jnp.where(row == seq - 1, 0.0, pltpu.roll(u, seq - 1, axis=0))
    conv = cw_ref[0:1, :] * u_prev + cw_ref[1:2, :] * u + cw_ref[2:3, :] * u_next
    zb = jnp.dot(h, wbf_ref[3], preferred_element_type=F32)
    zs_ref[...] = (zb * _sigmoid(zb)).astype(zs_ref.dtype)
    bg = jnp.dot(h, wbf_ref[0], preferred_element_type=F32)
    y_ref[...] = (bg * conv).astype(y_ref.dtype)


def _convproj(hn, w, cw, col_start, seq, n_ch, out_weights, tnc=256):
    t, d = hn.shape
    batch = t // seq
    blk0 = col_start // tnc
    per_slice = n_ch // tnc
    n_steps = per_slice * batch

    def w_spec(k):
        return pl.BlockSpec((d, tnc), lambda c, b, k=k: (0, blk0 + k * per_slice + c))

    def slab_spec(a):
        rows = max(a.shape[0] // n_steps, BF16_SUBLANES)
        last = a.shape[0] // rows - 1
        return pl.BlockSpec((rows, a.shape[1]),
                            lambda c, b: (jnp.minimum(c * batch + b, last), 0))

    slabs = [slab_spec(a) for a in out_weights]
    out_spec = pl.BlockSpec((seq, tnc), lambda c, b: (b, c))
    out_sds = jax.ShapeDtypeStruct((t, n_ch), BF16)
    return pl.pallas_call(
        _convproj_kernel,
        grid=(per_slice, batch),
        in_specs=[pl.BlockSpec((seq, d), lambda c, b: (b, 0)),
                  w_spec(0), w_spec(1), w_spec(2), w_spec(3),
                  pl.BlockSpec((cw.shape[0], tnc), lambda c, b: (0, c))] + slabs,
        out_specs=[out_spec, out_spec] + slabs,
        out_shape=[out_sds, out_sds] + [jax.ShapeDtypeStruct(a.shape, BF16)
                                        for a in out_weights],
        scratch_shapes=[pltpu.VMEM((4, d, tnc), BF16)],
        compiler_params=pltpu.CompilerParams(
            dimension_semantics=("arbitrary", "arbitrary"),
            vmem_limit_bytes=VMEM_LIMIT_BYTES),
        name="convproj",
    )(hn, w, w, w, w, cw, *out_weights)


def _hgrn_kernel(qs_ref, v_ref, hif_ref, lof_ref, kf_ref, hib_ref, lob_ref, kb_ref, zs_ref,
                 nw_ref, o_ref, acc_ref, qin_ref, kv_ref, dec_ref, sprev_ref):
    seq = qs_ref.shape[1]
    c = CHUNK
    n_chunks = seq // c
    head0 = pl.program_id(1) * HEADS_PER_STEP
    mid = c // 2 - 1

    row = lax.broadcasted_iota(jnp.int32, (c, c), 0)
    col = lax.broadcasted_iota(jnp.int32, (c, c), 1)
    masks = (col <= row, col >= row)
    row2 = lax.broadcasted_iota(jnp.int32, (c, 2 * c), 0)
    col2 = lax.broadcasted_iota(jnp.int32, (c, 2 * c), 1) % c
    tris = ((col2 <= row2).astype(BF16), (col2 >= row2).astype(BF16))
    hi_refs = (hif_ref, hib_ref)
    lo_refs = (lof_ref, lob_ref)
    k_refs = (kf_ref, kb_ref)
    mid_rows = (mid, c - 1 - mid)
    last_rows = (c - 1, 0)

    def rows(j):
        return slice(j * c, (j + 1) * c)

    def intra(hh):
        qs = qs_ref[hh].astype(F32)
        v = v_ref[hh]
        ks, bs = [], []
        for d in range(2):
            hi = hi_refs[d][hh]
            lo = lo_refs[d][hh]
            ks.append(k_refs[d][hh].astype(F32))
            bs.append([_dot(tris[d], jnp.concatenate([hi[rows(j)], lo[rows(j)]], axis=0))
                       for j in range(n_chunks)])
        ss = ([], [])
        for j in range(n_chunks):
            for d in range(2):
                b = bs[d][j]
                b_mid = b[mid_rows[d]:mid_rows[d] + 1, :]
                b_last = b[last_rows[d]:last_rows[d] + 1, :]
                q_rel = qs[rows(j)] * jnp.exp(b - b_mid)
                k_rel = ks[d][rows(j)] * jnp.exp(b_mid - b)
                ss[d].append(_dot(q_rel.astype(BF16), k_rel.astype(BF16), ((1,), (1,))))
                q_in = q_rel * jnp.exp(b_mid)
                k_st = k_rel * jnp.exp(b_last - b_mid)
                qin_ref[hh, d, rows(j), :] = q_in.astype(BF16)
                kv_ref[hh, d, j] = _dot(v[rows(j)], k_st.astype(BF16), ((0,), (0,)))
                dec_ref[hh, d, j] = jnp.broadcast_to(jnp.exp(b_last), (8, LANES))
        for j in range(n_chunks):
            p = jnp.where(masks[0], ss[0][j], 0.0) + jnp.where(masks[1], ss[1][j], 0.0)
            acc_ref[hh, rows(j), :] = _dot(p.astype(BF16), v[rows(j)])

    def scan(hh):
        st_f = jnp.zeros((LANES, LANES), F32)
        st_b = st_f
        for i in range(n_chunks):
            nb = n_chunks - 1 - i
            sprev_ref[hh, i, :, 0:LANES] = st_f.astype(BF16)
            sprev_ref[hh, nb, :, LANES:2 * LANES] = st_b.astype(BF16)
            st_f = st_f * dec_ref[hh, 0, i][0:1, :] + kv_ref[hh, 0, i]
            st_b = st_b * dec_ref[hh, 1, nb][0:1, :] + kv_ref[hh, 1, nb]

    def finish(hh):
        inter = []
        for j in range(n_chunks):
            qin = jnp.concatenate([qin_ref[hh, 0, rows(j), :], qin_ref[hh, 1, rows(j), :]],
                                  axis=1)
            inter.append(_dot(qin, sprev_ref[hh, j], ((1,), (1,))))
        o = acc_ref[hh] + jnp.concatenate(inter, axis=0)
        ms = jnp.mean(o * o, axis=-1, keepdims=True)
        nw = nw_ref[pl.ds(head0 + hh, 1), :]
        o_ref[:, hh * LANES:(hh + 1) * LANES] = (
            o * lax.rsqrt(ms + EPS) * nw * zs_ref[hh].astype(F32)).astype(o_ref.dtype)

    for stage in (intra, scan, finish):
        for hh in range(HEADS_PER_STEP):
            stage(hh)


def _hgrn(sil, v, hi, lo, k, nw, batch, seq, n_heads):
    t = batch * seq
    n_chunks = seq // CHUNK
    hps = HEADS_PER_STEP
    groups = n_heads // hps

    def spec(second_half):
        return pl.BlockSpec((hps, seq, LANES), lambda b, g: (second_half * groups + g, b, 0))

    fwd, bwd = spec(0), spec(1)
    return pl.pallas_call(
        _hgrn_kernel,
        grid=(batch, n_heads // hps),
        in_specs=[fwd, fwd, fwd, fwd, fwd, bwd, bwd, bwd, bwd,
                  pl.BlockSpec(nw.shape, lambda b, g: (0, 0))],
        out_specs=pl.BlockSpec((seq, hps * LANES), lambda b, g: (b, g)),
        out_shape=jax.ShapeDtypeStruct((t, n_heads * LANES), BF16),
        scratch_shapes=[
            pltpu.VMEM((hps, seq, LANES), F32),
            pltpu.VMEM((hps, 2, seq, LANES), BF16),
            pltpu.VMEM((hps, 2, n_chunks, LANES, LANES), F32),
            pltpu.VMEM((hps, 2, n_chunks, 8, LANES), F32),
            pltpu.VMEM((hps, n_chunks, LANES, 2 * LANES), BF16),
        ],
        compiler_params=pltpu.CompilerParams(
            dimension_semantics=("arbitrary", "arbitrary"),
            vmem_limit_bytes=VMEM_LIMIT_BYTES),
        name="hgrn2",
    )(sil, v, hi, lo, k, hi, lo, k, sil, nw)


def _out_kernel(oa_ref, y_ref, zs_ref, x_ref, p_ref, wo_ref, wg_ref, wp_ref, cnw_ref, fw_ref,
                o_ref):
    d_a = oa_ref.shape[1]
    y = y_ref[...].astype(F32)
    ms_b = jnp.mean(y * y, axis=-1, keepdims=True)
    ob = (y * zs_ref[...].astype(F32) * (lax.rsqrt(ms_b + EPS) * cnw_ref[...])).astype(BF16)
    h1 = (x_ref[...]
          + jnp.dot(oa_ref[...], wo_ref[0:d_a, :], preferred_element_type=F32)
          + jnp.dot(ob, wo_ref[d_a:, :], preferred_element_type=F32))
    gate = _sigmoid(jnp.dot(h1.astype(BF16), wg_ref[...], preferred_element_type=F32))
    ple = jnp.dot(p_ref[...].astype(BF16), wp_ref[...], preferred_element_type=F32)
    h2 = h1 + ple * gate
    ms = jnp.mean(h2 * h2, axis=-1, keepdims=True)
    o_ref[...] = h2 * lax.rsqrt(ms + EPS) * fw_ref[...]


def _out(oa, y, zs, x2d, p2d, wo, wg, wp, cnw, fw, tm=256):
    t, d = x2d.shape

    def resident(shape):
        return pl.BlockSpec(shape, lambda i: (0, 0), pipeline_mode=pl.Buffered(1))

    def rows(a):
        return pl.BlockSpec((tm, a.shape[1]), lambda i: (i, 0))

    return pl.pallas_call(
        _out_kernel,
        grid=(t // tm,),
        in_specs=[rows(oa), rows(y), rows(zs), rows(x2d), rows(p2d),
                  resident(wo.shape), resident(wg.shape), resident(wp.shape),
                  resident((1, d)), resident((1, d))],
        out_specs=pl.BlockSpec((tm, d), lambda i: (i, 0)),
        out_shape=jax.ShapeDtypeStruct((t, d), F32),
        compiler_params=pltpu.CompilerParams(
            dimension_semantics=("arbitrary",),
            vmem_limit_bytes=VMEM_LIMIT_BYTES),
        name="outproj",
    )(oa, y, zs, x2d, p2d, wo, wg, wp, cnw.reshape(1, d), fw.reshape(1, d))


def kernel(x, p, norm_w, w_in, lb_theta, hgrn_norm_w, conv_w, conv_norm_w, w_out, w_ple,
           w_ple_gate, final_norm_w):
    batch, seq, d = x.shape
    depth = p.shape[0]
    assert depth == 1 and lb_theta.shape[1] == 2
    t = batch * seq
    n_heads = d // LANES

    x2d = x.reshape(t, d)
    hn = _rmsnorm(x2d, norm_w[0])
    w = w_in[0]

    theta2 = jnp.transpose(lb_theta, (1, 0, 2)).reshape(2, 2 * d)
    hi, lo, k, sil, v = _inproj_hgrn(hn, w, theta2, 2 * d, (0 * d, 4 * d), 1 * d, d)
    o_a = _hgrn(sil, v, hi, lo, k, hgrn_norm_w[0].reshape(n_heads, LANES), batch, seq, n_heads)

    y_b, zs_b, wo, wg, wp = _convproj(hn, w, conv_w[0], 5 * d, seq, d,
                                      (w_out[0], w_ple_gate[0], w_ple[0]))

    out = _out(o_a, y_b, zs_b, x2d, p[0].reshape(t, -1), wo, wg, wp,
               conv_norm_w[0], final_norm_w)
    return out.reshape(batch, seq, d)
```

```python
import jax
import jax.numpy as jnp
from jax import lax
from jax.experimental import pallas as pl
from jax.experimental.pallas import tpu as pltpu

EPS = 1e-6
LANES = 128
BF16_SUBLANES = 16
CHUNK = 64
HEADS_PER_STEP = 2
VMEM_CAPACITY_BYTES = 64 * 1024 * 1024
VMEM_LIMIT_BYTES = VMEM_CAPACITY_BYTES - 4 * 1024 * 1024

F32 = jnp.float32
BF16 = jnp.bfloat16


def _sigmoid(x):
    return 0.5 * jnp.tanh(0.5 * x) + 0.5


def _dot(a, b, dims=((1,), (0,))):
    return lax.dot_general(a, b, (dims, ((), ())), preferred_element_type=F32)


def _rmsnorm_kernel(x_ref, w_ref, o_ref):
    x = x_ref[...]
    ms = jnp.mean(x * x, axis=-1, keepdims=True)
    o_ref[...] = (x * lax.rsqrt(ms + EPS) * w_ref[...]).astype(o_ref.dtype)


def _rmsnorm(x2d, w, tm=1024):
    t, d = x2d.shape
    return pl.pallas_call(
        _rmsnorm_kernel,
        grid=(t // tm,),
        in_specs=[pl.BlockSpec((tm, d), lambda i: (i, 0)),
                  pl.BlockSpec((1, d), lambda i: (0, 0))],
        out_specs=pl.BlockSpec((tm, d), lambda i: (i, 0)),
        out_shape=jax.ShapeDtypeStruct((t, d), BF16),
        name="rmsnorm",
    )(x2d, w.reshape(1, d))


def _store_heads(o_ref, val):
    for g in range(val.shape[1] // LANES):
        o_ref[g] = val[:, g * LANES:(g + 1) * LANES].astype(o_ref.dtype)


def _inproj_hgrn_kernel(h_ref, wg_ref, ws_ref, wv_ref, theta_ref, hi_ref, lo_ref, k_ref, s_ref,
                        v_ref, wbf_ref, acc_ref):
    tg = wg_ref.shape[1]
    ts = ws_ref.shape[1]

    @pl.when(pl.program_id(1) == 0)
    def _():
        wbf_ref[:, 0:tg] = wg_ref[...].astype(BF16)
        wbf_ref[:, tg:tg + ts] = ws_ref[...].astype(BF16)
        wbf_ref[:, tg + ts:] = wv_ref[...].astype(BF16)

    h = h_ref[...]
    acc_ref[...] = jnp.dot(h, wbf_ref[:, 0:tg], preferred_element_type=F32)
    t0 = theta_ref[0:1, :]
    t1 = theta_ref[1:2, :]
    m = jnp.maximum(t0, t1)
    e0 = jnp.exp(t0 - m)
    e1 = jnp.exp(t1 - m)
    lb = e0 / (e0 + e1)
    f = 0.5 * (1.0 + lb) + (0.5 * (1.0 - lb)) * jnp.tanh(0.5 * acc_ref[...])
    lf = jnp.log(f)
    hi = lf.astype(BF16)
    _store_heads(hi_ref, hi)
    _store_heads(lo_ref, lf - hi.astype(F32))
    _store_heads(k_ref, 1.0 - f)
    acc = jnp.dot(h, wbf_ref[:, tg:tg + ts], preferred_element_type=F32)
    _store_heads(s_ref, acc * _sigmoid(acc))
    _store_heads(v_ref, jnp.dot(h, wbf_ref[:, tg + ts:], preferred_element_type=F32))


def _inproj_hgrn(hn, w, theta2, gate_col_start, silu_col_starts, value_col_start, n_cols,
                 tm=512, tg=512):
    t, d = hn.shape
    n_blocks = 2 * n_cols // tg
    half = n_blocks // 2
    tv = n_cols // n_blocks
    g0 = gate_col_start // tg
    s0, s1 = (cs // tg for cs in silu_col_starts)
    v0 = value_col_start // tv

    def out(width, n_slices):
        return (pl.BlockSpec((width // LANES, tm, LANES), lambda c, i: (c, i, 0)),
                jax.ShapeDtypeStruct((n_slices * n_cols // LANES, t, LANES), BF16))

    outs = [out(tg, 2)] * 4 + [out(tv, 1)]
    return pl.pallas_call(
        _inproj_hgrn_kernel,
        grid=(n_blocks, t // tm),
        in_specs=[pl.BlockSpec((tm, d), lambda c, i: (i, 0)),
                  pl.BlockSpec((d, tg), lambda c, i: (0, g0 + c)),
                  pl.BlockSpec((d, tg),
                               lambda c, i: (0, jnp.where(c < half, s0 + c, s1 + c - half))),
                  pl.BlockSpec((d, tv), lambda c, i: (0, v0 + c)),
                  pl.BlockSpec((theta2.shape[0], tg), lambda c, i: (0, c))],
        out_specs=[o[0] for o in outs],
        out_shape=[o[1] for o in outs],
        scratch_shapes=[pltpu.VMEM((d, 2 * tg + tv), BF16), pltpu.VMEM((tm, tg), F32)],
        compiler_params=pltpu.CompilerParams(
            dimension_semantics=("arbitrary", "arbitrary"),
            vmem_limit_bytes=VMEM_LIMIT_BYTES),
        name="inproj_hgrn",
    )(hn, w, w, w, theta2)


def _convproj_kernel(h_ref, wb_ref, wc_ref, wh_ref, wz_ref, cw_ref, wo_ref, wg_ref, wp_ref,
                     y_ref, zs_ref, wo_bf_ref, wg_bf_ref, wp_bf_ref, wbf_ref):
    seq = h_ref.shape[0]
    wo_bf_ref[...] = wo_ref[...].astype(BF16)
    wg_bf_ref[...] = wg_ref[...].astype(BF16)
    wp_bf_ref[...] = wp_ref[...].astype(BF16)

    @pl.when(pl.program_id(1) == 0)
    def _():
        for k, w_ref in enumerate((wb_ref, wc_ref, wh_ref, wz_ref)):
            wbf_ref[k] = w_ref[...].astype(BF16)

    h = h_ref[...]
    cg = jnp.dot(h, wbf_ref[1], preferred_element_type=F32)
    hb = jnp.dot(h, wbf_ref[2], preferred_element_type=F32)
    u = cg * hb
    row = lax.broadcasted_iota(jnp.int32, u.shape, 0)
    u_prev = jnp.where(row == 0, 0.0, pltpu.roll(u, 1, axis=0))
    u_next = jnp.where(row == seq - 1, 0.0, pltpu.roll(u, seq - 1, axis=0))
    conv = cw_ref[0:1, :] * u_prev + cw_ref[1:2, :] * u + cw_ref[2:3, :] * u_next
    zb = jnp.dot(h, wbf_ref[3], preferred_element_type=F32)
    zs_ref[...] = (zb * _sigmoid(zb)).astype(zs_ref.dtype)
    bg = jnp.dot(h, wbf_ref[0], preferred_element_type=F32)
    y_ref[...] = (bg * conv).astype(y_ref.dtype)


def _convproj(hn, w, cw, col_start, seq, n_ch, out_weights, tnc=256):
    t, d = hn.shape
    batch = t // seq
    blk0 = col_start // tnc
    per_slice = n_ch // tnc
    n_steps = per_slice * batch

    def w_spec(k):
        return pl.BlockSpec((d, tnc), lambda c, b, k=k: (0, blk0 + k * per_slice + c))

    def slab_spec(a):
        rows = max(a.shape[0] // n_steps, BF16_SUBLANES)
        last = a.shape[0] // rows - 1
        return pl.BlockSpec((rows, a.shape[1]),
                            lambda c, b: (jnp.minimum(c * batch + b, last), 0))

    slabs = [slab_spec(a) for a in out_weights]
    out_spec = pl.BlockSpec((seq, tnc), lambda c, b: (b, c))
    out_sds = jax.ShapeDtypeStruct((t, n_ch), BF16)
    return pl.pallas_call(
        _convproj_kernel,
        grid=(per_slice, batch),
        in_specs=[pl.BlockSpec((seq, d), lambda c, b: (b, 0)),
                  w_spec(0), w_spec(1), w_spec(2), w_spec(3),
                  pl.BlockSpec((cw.shape[0], tnc), lambda c, b: (0, c))] + slabs,
        out_specs=[out_spec, out_spec] + slabs,
        out_shape=[out_sds, out_sds] + [jax.ShapeDtypeStruct(a.shape, BF16)
                                        for a in out_weights],
        scratch_shapes=[pltpu.VMEM((4, d, tnc), BF16)],
        compiler_params=pltpu.CompilerParams(
            dimension_semantics=("arbitrary", "arbitrary"),
            vmem_limit_bytes=VMEM_LIMIT_BYTES),
        name="convproj",
    )(hn, w, w, w, w, cw, *out_weights)


def _hgrn_kernel(qs_ref, v_ref, hif_ref, lof_ref, kf_ref, hib_ref, lob_ref, kb_ref, zs_ref,
                 nw_ref, o_ref, acc_ref, qin_ref, kv_ref, dec_ref, sprev_ref):
    seq = qs_ref.shape[1]
    c = CHUNK
    n_chunks = seq // c
    head0 = pl.program_id(1) * HEADS_PER_STEP
    mid = c // 2 - 1

    row = lax.broadcasted_iota(jnp.int32, (c, c), 0)
    col = lax.broadcasted_iota(jnp.int32, (c, c), 1)
    masks = (col <= row, col >= row)
    row2 = lax.broadcasted_iota(jnp.int32, (c, 2 * c), 0)
    col2 = lax.broadcasted_iota(jnp.int32, (c, 2 * c), 1) % c
    tris = ((col2 <= row2).astype(BF16), (col2 >= row2).astype(BF16))
    hi_refs = (hif_ref, hib_ref)
    lo_refs = (lof_ref, lob_ref)
    k_refs = (kf_ref, kb_ref)
    mid_rows = (mid, c - 1 - mid)
    last_rows = (c - 1, 0)

    def rows(j):
        return slice(j * c, (j + 1) * c)

    def intra(hh):
        qs = qs_ref[hh].astype(F32)
        v = v_ref[hh]
        ks, bs = [], []
        for d in range(2):
            hi = hi_refs[d][hh]
            lo = lo_refs[d][hh]
            ks.append(k_refs[d][hh].astype(F32))
            bs.append([_dot(tris[d], jnp.concatenate([hi[rows(j)], lo[rows(j)]], axis=0))
                       for j in range(n_chunks)])
        ss = ([], [])
        for j in range(n_chunks):
            for d in range(2):
                b = bs[d][j]
                b_mid = b[mid_rows[d]:mid_rows[d] + 1, :]
                b_last = b[last_rows[d]:last_rows[d] + 1, :]
                q_rel = qs[rows(j)] * jnp.exp(b - b_mid)
                k_rel = ks[d][rows(j)] * jnp.exp(b_mid - b)
                ss[d].append(_dot(q_rel.astype(BF16), k_rel.astype(BF16), ((1,), (1,))))
                q_in = q_rel * jnp.exp(b_mid)
                k_st = k_rel * jnp.exp(b_last - b_mid)
                qin_ref[hh, d, rows(j), :] = q_in.astype(BF16)
                kv_ref[hh, d, j] = _dot(v[rows(j)], k_st.astype(BF16), ((0,), (0,)))
                dec_ref[hh, d, j] = jnp.broadcast_to(jnp.exp(b_last), (8, LANES))
        for j in range(n_chunks):
            p = jnp.where(masks[0], ss[0][j], 0.0) + jnp.where(masks[1], ss[1][j], 0.0)
            acc_ref[hh, rows(j), :] = _dot(p.astype(BF16), v[rows(j)])

    def scan(hh):
        st_f = jnp.zeros((LANES, LANES), F32)
        st_b = st_f
        for i in range(n_chunks):
            nb = n_chunks - 1 - i
            sprev_ref[hh, i, :, 0:LANES] = st_f.astype(BF16)
            sprev_ref[hh, nb, :, LANES:2 * LANES] = st_b.astype(BF16)
            st_f = st_f * dec_ref[hh, 0, i][0:1, :] + kv_ref[hh, 0, i]
            st_b = st_b * dec_ref[hh, 1, nb][0:1, :] + kv_ref[hh, 1, nb]

    def finish(hh):
        inter = []
        for j in range(n_chunks):
            qin = jnp.concatenate([qin_ref[hh, 0, rows(j), :], qin_ref[hh, 1, rows(j), :]],
                                  axis=1)
            inter.append(_dot(qin, sprev_ref[hh, j], ((1,), (1,))))
        o = acc_ref[hh] + jnp.concatenate(inter, axis=0)
        ms = jnp.mean(o * o, axis=-1, keepdims=True)
        nw = nw_ref[pl.ds(head0 + hh, 1), :]
        o_ref[:, hh * LANES:(hh + 1) * LANES] = (
            o * lax.rsqrt(ms + EPS) * nw * zs_ref[hh].astype(F32)).astype(o_ref.dtype)

    for stage in (intra, scan, finish):
        for hh in range(HEADS_PER_STEP):
            stage(hh)


def _hgrn(sil, v, hi, lo, k, nw, batch, seq, n_heads):
    t = batch * seq
    n_chunks = seq // CHUNK
    hps = HEADS_PER_STEP
    groups = n_heads // hps

    def spec(second_half):
        return pl.BlockSpec((hps, seq, LANES), lambda b, g: (second_half * groups + g, b, 0))

    fwd, bwd = spec(0), spec(1)
    return pl.pallas_call(
        _hgrn_kernel,
        grid=(batch, n_heads // hps),
        in_specs=[fwd, fwd, fwd, fwd, fwd, bwd, bwd, bwd, bwd,
                  pl.BlockSpec(nw.shape, lambda b, g: (0, 0))],
        out_specs=pl.BlockSpec((seq, hps * LANES), lambda b, g: (b, g)),
        out_shape=jax.ShapeDtypeStruct((t, n_heads * LANES), BF16),
        scratch_shapes=[
            pltpu.VMEM((hps, seq, LANES), F32),
            pltpu.VMEM((hps, 2, seq, LANES), BF16),
            pltpu.VMEM((hps, 2, n_chunks, LANES, LANES), F32),
            pltpu.VMEM((hps, 2, n_chunks, 8, LANES), F32),
            pltpu.VMEM((hps, n_chunks, LANES, 2 * LANES), BF16),
        ],
        compiler_params=pltpu.CompilerParams(
            dimension_semantics=("arbitrary", "arbitrary"),
            vmem_limit_bytes=VMEM_LIMIT_BYTES),
        name="hgrn2",
    )(sil, v, hi, lo, k, hi, lo, k, sil, nw)


def _out_kernel(oa_ref, y_ref, zs_ref, x_ref, p_ref, wo_ref, wg_ref, wp_ref, cnw_ref, fw_ref,
                o_ref):
    d_a = oa_ref.shape[1]
    y = y_ref[...].astype(F32)
    ms_b = jnp.mean(y * y, axis=-1, keepdims=True)
    ob = (y * zs_ref[...].astype(F32) * (lax.rsqrt(ms_b + EPS) * cnw_ref[...])).astype(BF16)
    h1 = (x_ref[...]
          + jnp.dot(oa_ref[...], wo_ref[0:d_a, :], preferred_element_type=F32)
          + jnp.dot(ob, wo_ref[d_a:, :], preferred_element_type=F32))
    gate = _sigmoid(jnp.dot(h1.astype(BF16), wg_ref[...], preferred_element_type=F32))
    ple = jnp.dot(p_ref[...].astype(BF16), wp_ref[...], preferred_element_type=F32)
    h2 = h1 + ple * gate
    ms = jnp.mean(h2 * h2, axis=-1, keepdims=True)
    o_ref[...] = h2 * lax.rsqrt(ms + EPS) * fw_ref[...]


def _out(oa, y, zs, x2d, p2d, wo, wg, wp, cnw, fw, tm=256):
    t, d = x2d.shape

    def resident(shape):
        return pl.BlockSpec(shape, lambda i: (0, 0), pipeline_mode=pl.Buffered(1))

    def rows(a):
        return pl.BlockSpec((tm, a.shape[1]), lambda i: (i, 0))

    return pl.pallas_call(
        _out_kernel,
        grid=(t // tm,),
        in_specs=[rows(oa), rows(y), rows(zs), rows(x2d), rows(p2d),
                  resident(wo.shape), resident(wg.shape), resident(wp.shape),
                  resident((1, d)), resident((1, d))],
        out_specs=pl.BlockSpec((tm, d), lambda i: (i, 0)),
        out_shape=jax.ShapeDtypeStruct((t, d), F32),
        compiler_params=pltpu.CompilerParams(
            dimension_semantics=("arbitrary",),
            vmem_limit_bytes=VMEM_LIMIT_BYTES),
        name="outproj",
    )(oa, y, zs, x2d, p2d, wo, wg, wp, cnw.reshape(1, d), fw.reshape(1, d))


def kernel(x, p, norm_w, w_in, lb_theta, hgrn_norm_w, conv_w, conv_norm_w, w_out, w_ple,
           w_ple_gate, final_norm_w):
    batch, seq, d = x.shape
    depth = p.shape[0]
    assert depth == 1 and lb_theta.shape[1] == 2
    t = batch * seq
    n_heads = d // LANES

    x2d = x.reshape(t, d)
    hn = _rmsnorm(x2d, norm_w[0])
    w = w_in[0]

    theta2 = jnp.transpose(lb_theta, (1, 0, 2)).reshape(2, 2 * d)
    hi, lo, k, sil, v = _inproj_hgrn(hn, w, theta2, 2 * d, (0 * d, 4 * d), 1 * d, d)
    o_a = _hgrn(sil, v, hi, lo, k, hgrn_norm_w[0].reshape(n_heads, LANES), batch, seq, n_heads)

    y_b, zs_b, wo, wg, wp = _convproj(hn, w, conv_w[0], 5 * d, seq, d,
                                      (w_out[0], w_ple_gate[0], w_ple[0]))

    out = _out(o_a, y_b, zs_b, x2d, p[0].reshape(t, -1), wo, wg, wp,
               conv_norm_w[0], final_norm_w)
    return out.reshape(batch, seq, d)
```

```python
import jax
import jax.numpy as jnp
from jax import lax
from jax.experimental import pallas as pl
from jax.experimental.pallas import tpu as pltpu

EPS = 1e-6
LANES = 128
BF16_SUBLANES = 16
CHUNK = 64
HEADS_PER_STEP = 2
VMEM_CAPACITY_BYTES = 64 * 1024 * 1024
VMEM_LIMIT_BYTES = VMEM_CAPACITY_BYTES - 4 * 1024 * 1024

F32 = jnp.float32
BF16 = jnp.bfloat16


def _sigmoid(x):
    return 0.5 * jnp.tanh(0.5 * x) + 0.5


def _dot(a, b, dims=((1,), (0,))):
    return lax.dot_general(a, b, (dims, ((), ())), preferred_element_type=F32)


def _rmsnorm_kernel(x_ref, w_ref, o_ref):
    x = x_ref[...]
    ms = jnp.mean(x * x, axis=-1, keepdims=True)
    o_ref[...] = (x * lax.rsqrt(ms + EPS) * w_ref[...]).astype(o_ref.dtype)


def _rmsnorm(x2d, w, tm=1024):
    t, d = x2d.shape
    return pl.pallas_call(
        _rmsnorm_kernel,
        grid=(t // tm,),
        in_specs=[pl.BlockSpec((tm, d), lambda i: (i, 0)),
                  pl.BlockSpec((1, d), lambda i: (0, 0))],
        out_specs=pl.BlockSpec((tm, d), lambda i: (i, 0)),
        out_shape=jax.ShapeDtypeStruct((t, d), BF16),
        name="rmsnorm",
    )(x2d, w.reshape(1, d))


def _store_heads(o_ref, val):
    for g in range(val.shape[1] // LANES):
        o_ref[g] = val[:, g * LANES:(g + 1) * LANES].astype(o_ref.dtype)


def _inproj_hgrn_kernel(h_ref, wg_ref, ws_ref, wv_ref, theta_ref, hi_ref, lo_ref, k_ref, s_ref,
                        v_ref):
    h = h_ref[...]
    acc = jnp.dot(h, wg_ref[...].astype(BF16), preferred_element_type=F32)
    t0 = theta_ref[0:1, :]
    t1 = theta_ref[1:2, :]
    m = jnp.maximum(t0, t1)
    e0 = jnp.exp(t0 - m)
    e1 = jnp.exp(t1 - m)
    lb = e0 / (e0 + e1)
    f = 0.5 * (1.0 + lb) + (0.5 * (1.0 - lb)) * jnp.tanh(0.5 * acc)
    lf = jnp.log(f)
    hi = lf.astype(BF16)
    _store_heads(hi_ref, hi)
    _store_heads(lo_ref, lf - hi.astype(F32))
    _store_heads(k_ref, 1.0 - f)
    acc = jnp.dot(h, ws_ref[...].astype(BF16), preferred_element_type=F32)
    _store_heads(s_ref, acc * _sigmoid(acc))
    _store_heads(v_ref, jnp.dot(h, wv_ref[...].astype(BF16), preferred_element_type=F32))


def _inproj_hgrn(hn, w, theta2, gate_col_start, silu_col_starts, value_col_start, n_cols,
                 tm=2048, tg=512):
    t, d = hn.shape
    n_blocks = 2 * n_cols // tg
    half = n_blocks // 2
    tv = n_cols // n_blocks
    g0 = gate_col_start // tg
    s0, s1 = (cs // tg for cs in silu_col_starts)
    v0 = value_col_start // tv

    def out(width, n_slices):
        return (pl.BlockSpec((width // LANES, tm, LANES), lambda i, c: (c, i, 0)),
                jax.ShapeDtypeStruct((n_slices * n_cols // LANES, t, LANES), BF16))

    outs = [out(tg, 2)] * 4 + [out(tv, 1)]
    return pl.pallas_call(
        _inproj_hgrn_kernel,
        grid=(t // tm, n_blocks),
        in_specs=[pl.BlockSpec((tm, d), lambda i, c: (i, 0), pipeline_mode=pl.Buffered(1)),
                  pl.BlockSpec((d, tg), lambda i, c: (0, g0 + c)),
                  pl.BlockSpec((d, tg),
                               lambda i, c: (0, jnp.where(c < half, s0 + c, s1 + c - half))),
                  pl.BlockSpec((d, tv), lambda i, c: (0, v0 + c)),
                  pl.BlockSpec((theta2.shape[0], tg), lambda i, c: (0, c))],
        out_specs=[o[0] for o in outs],
        out_shape=[o[1] for o in outs],
        compiler_params=pltpu.CompilerParams(
            dimension_semantics=("arbitrary", "arbitrary"),
            vmem_limit_bytes=VMEM_LIMIT_BYTES),
        name="inproj_hgrn",
    )(hn, w, w, w, theta2)


def _convproj_kernel(h_ref, wb_ref, wc_ref, wh_ref, wz_ref, cw_ref, wo_ref, wg_ref, wp_ref,
                     y_ref, zs_ref, wo_bf_ref, wg_bf_ref, wp_bf_ref, wbf_ref):
    seq = h_ref.shape[0]
    wo_bf_ref[...] = wo_ref[...].astype(BF16)
    wg_bf_ref[...] = wg_ref[...].astype(BF16)
    wp_bf_ref[...] = wp_ref[...].astype(BF16)

    @pl.when(pl.program_id(1) == 0)
    def _():
        for k, w_ref in enumerate((wb_ref, wc_ref, wh_ref, wz_ref)):
            wbf_ref[k] = w_ref[...].astype(BF16)

    h = h_ref[...]
    cg = jnp.dot(h, wbf_ref[1], preferred_element_type=F32)
    hb = jnp.dot(h, wbf_ref[2], preferred_element_type=F32)
    u = cg * hb
    row = lax.broadcasted_iota(jnp.int32, u.shape, 0)
    u_prev = jnp.where(row == 0, 0.0, pltpu.roll(u, 1, axis=0))
    u_next = jnp.where(row == seq - 1, 0.0, pltpu.roll(u, seq - 1, axis=0))
    conv = cw_ref[0:1, :] * u_prev + cw_ref[1:2, :] * u + cw_ref[2:3, :] * u_next
    zb = jnp.dot(h, wbf_ref[3], preferred_element_type=F32)
    zs_ref[...] = (zb * _sigmoid(zb)).astype(zs_ref.dtype)
    bg = jnp.dot(h, wbf_ref[0], preferred_element_type=F32)
    y_ref[...] = (bg * conv).astype(y_ref.dtype)


def _convproj(hn, w, cw, col_start, seq, n_ch, out_weights, tnc=256):
    t, d = hn.shape
    batch = t // seq
    blk0 = col_start // tnc
    per_slice = n_ch // tnc
    n_steps = per_slice * batch

    def w_spec(k):
        return pl.BlockSpec((d, tnc), lambda c, b, k=k: (0, blk0 + k * per_slice + c))

    def slab_spec(a):
        rows = max(a.shape[0] // n_steps, BF16_SUBLANES)
        last = a.shape[0] // rows - 1
        return pl.BlockSpec((rows, a.shape[1]),
                            lambda c, b: (jnp.minimum(c * batch + b, last), 0))

    slabs = [slab_spec(a) for a in out_weights]
    out_spec = pl.BlockSpec((seq, tnc), lambda c, b: (b, c))
    out_sds = jax.ShapeDtypeStruct((t, n_ch), BF16)
    return pl.pallas_call(
        _convproj_kernel,
        grid=(per_slice, batch),
        in_specs=[pl.BlockSpec((seq, d), lambda c, b: (b, 0)),
                  w_spec(0), w_spec(1), w_spec(2), w_spec(3),
                  pl.BlockSpec((cw.shape[0], tnc), lambda c, b: (0, c))] + slabs,
        out_specs=[out_spec, out_spec] + slabs,
        out_shape=[out_sds, out_sds] + [jax.ShapeDtypeStruct(a.shape, BF16)
                                        for a in out_weights],
        scratch_shapes=[pltpu.VMEM((4, d, tnc), BF16)],
        compiler_params=pltpu.CompilerParams(
            dimension_semantics=("arbitrary", "arbitrary"),
            vmem_limit_bytes=VMEM_LIMIT_BYTES),
        name="convproj",
    )(hn, w, w, w, w, cw, *out_weights)


def _hgrn_kernel(qs_ref, v_ref, hif_ref, lof_ref, kf_ref, hib_ref, lob_ref, kb_ref, zs_ref,
                 nw_ref, o_ref, acc_ref, qin_ref, kv_ref, dec_ref, sprev_ref):
    seq = qs_ref.shape[1]
    c = CHUNK
    n_chunks = seq // c
    head0 = pl.program_id(1) * HEADS_PER_STEP
    mid = c // 2 - 1

    row = lax.broadcasted_iota(jnp.int32, (c, c), 0)
    col = lax.broadcasted_iota(jnp.int32, (c, c), 1)
    masks = (col <= row, col >= row)
    row2 = lax.broadcasted_iota(jnp.int32, (c, 2 * c), 0)
    col2 = lax.broadcasted_iota(jnp.int32, (c, 2 * c), 1) % c
    tris = ((col2 <= row2).astype(BF16), (col2 >= row2).astype(BF16))
    hi_refs = (hif_ref, hib_ref)
    lo_refs = (lof_ref, lob_ref)
    k_refs = (kf_ref, kb_ref)
    mid_rows = (mid, c - 1 - mid)
    last_rows = (c - 1, 0)

    def rows(j):
        return slice(j * c, (j + 1) * c)

    def intra(hh):
        qs = qs_ref[hh].astype(F32)
        v = v_ref[hh]
        ks, bs = [], []
        for d in range(2):
            hi = hi_refs[d][hh]
            lo = lo_refs[d][hh]
            ks.append(k_refs[d][hh].astype(F32))
            bs.append([_dot(tris[d], jnp.concatenate([hi[rows(j)], lo[rows(j)]], axis=0))
                       for j in range(n_chunks)])
        ss = ([], [])
        for j in range(n_chunks):
            for d in range(2):
                b = bs[d][j]
                b_mid = b[mid_rows[d]:mid_rows[d] + 1, :]
                b_last = b[last_rows[d]:last_rows[d] + 1, :]
                q_rel = qs[rows(j)] * jnp.exp(b - b_mid)
                k_rel = ks[d][rows(j)] * jnp.exp(b_mid - b)
                ss[d].append(_dot(q_rel.astype(BF16), k_rel.astype(BF16), ((1,), (1,))))
                q_in = q_rel * jnp.exp(b_mid)
                k_st = k_rel * jnp.exp(b_last - b_mid)
                qin_ref[hh, d, rows(j), :] = q_in.astype(BF16)
                kv_ref[hh, d, j] = _dot(v[rows(j)], k_st.astype(BF16), ((0,), (0,)))
                dec_ref[hh, d, j] = jnp.broadcast_to(jnp.exp(b_last), (8, LANES))
        for j in range(n_chunks):
            p = jnp.where(masks[0], ss[0][j], 0.0) + jnp.where(masks[1], ss[1][j], 0.0)
            acc_ref[hh, rows(j), :] = _dot(p.astype(BF16), v[rows(j)])

    def scan(hh):
        st_f = jnp.zeros((LANES, LANES), F32)
        st_b = st_f
        for i in range(n_chunks):
            nb = n_chunks - 1 - i
            sprev_ref[hh, i, :, 0:LANES] = st_f.astype(BF16)
            sprev_ref[hh, nb, :, LANES:2 * LANES] = st_b.astype(BF16)
            st_f = st_f * dec_ref[hh, 0, i][0:1, :] + kv_ref[hh, 0, i]
            st_b = st_b * dec_ref[hh, 1, nb][0:1, :] + kv_ref[hh, 1, nb]

    def finish(hh):
        inter = []
        for j in range(n_chunks):
            qin = jnp.concatenate([qin_ref[hh, 0, rows(j), :], qin_ref[hh, 1, rows(j), :]],
                                  axis=1)
            inter.append(_dot(qin, sprev_ref[hh, j], ((1,), (1,))))
        o = acc_ref[hh] + jnp.concatenate(inter, axis=0)
        ms = jnp.mean(o * o, axis=-1, keepdims=True)
        nw = nw_ref[pl.ds(head0 + hh, 1), :]
        o_ref[:, hh * LANES:(hh + 1) * LANES] = (
            o * lax.rsqrt(ms + EPS) * nw * zs_ref[hh].astype(F32)).astype(o_ref.dtype)

    for stage in (intra, scan, finish):
        for hh in range(HEADS_PER_STEP):
            stage(hh)


def _hgrn(sil, v, hi, lo, k, nw, batch, seq, n_heads):
    t = batch * seq
    n_chunks = seq // CHUNK
    hps = HEADS_PER_STEP
    groups = n_heads // hps

    def spec(second_half):
        return pl.BlockSpec((hps, seq, LANES), lambda b, g: (second_half * groups + g, b, 0))

    fwd, bwd = spec(0), spec(1)
    return pl.pallas_call(
        _hgrn_kernel,
        grid=(batch, n_heads // hps),
        in_specs=[fwd, fwd, fwd, fwd, fwd, bwd, bwd, bwd, bwd,
                  pl.BlockSpec(nw.shape, lambda b, g: (0, 0))],
        out_specs=pl.BlockSpec((seq, hps * LANES), lambda b, g: (b, g)),
        out_shape=jax.ShapeDtypeStruct((t, n_heads * LANES), BF16),
        scratch_shapes=[
            pltpu.VMEM((hps, seq, LANES), F32),
            pltpu.VMEM((hps, 2, seq, LANES), BF16),
            pltpu.VMEM((hps, 2, n_chunks, LANES, LANES), F32),
            pltpu.VMEM((hps, 2, n_chunks, 8, LANES), F32),
            pltpu.VMEM((hps, n_chunks, LANES, 2 * LANES), BF16),
        ],
        compiler_params=pltpu.CompilerParams(
            dimension_semantics=("arbitrary", "arbitrary"),
            vmem_limit_bytes=VMEM_LIMIT_BYTES),
        name="hgrn2",
    )(sil, v, hi, lo, k, hi, lo, k, sil, nw)


def _out_kernel(oa_ref, y_ref, zs_ref, x_ref, p_ref, wo_ref, wg_ref, wp_ref, cnw_ref, fw_ref,
                o_ref):
    d_a = oa_ref.shape[1]
    y = y_ref[...].astype(F32)
    ms_b = jnp.mean(y * y, axis=-1, keepdims=True)
    ob = (y * zs_ref[...].astype(F32) * (lax.rsqrt(ms_b + EPS) * cnw_ref[...])).astype(BF16)
    h1 = (x_ref[...]
          + jnp.dot(oa_ref[...], wo_ref[0:d_a, :], preferred_element_type=F32)
          + jnp.dot(ob, wo_ref[d_a:, :], preferred_element_type=F32))
    gate = _sigmoid(jnp.dot(h1.astype(BF16), wg_ref[...], preferred_element_type=F32))
    ple = jnp.dot(p_ref[...].astype(BF16), wp_ref[...], preferred_element_type=F32)
    h2 = h1 + ple * gate
    ms = jnp.mean(h2 * h2, axis=-1, keepdims=True)
    o_ref[...] = h2 * lax.rsqrt(ms + EPS) * fw_ref[...]


def _out(oa, y, zs, x2d, p2d, wo, wg, wp, cnw, fw, tm=256):
    t, d = x2d.shape

    def resident(shape):
        return pl.BlockSpec(shape, lambda i: (0, 0), pipeline_mode=pl.Buffered(1))

    def rows(a):
        return pl.BlockSpec((tm, a.shape[1]), lambda i: (i, 0))

    return pl.pallas_call(
        _out_kernel,
        grid=(t // tm,),
        in_specs=[rows(oa), rows(y), rows(zs), rows(x2d), rows(p2d),
                  resident(wo.shape), resident(wg.shape), resident(wp.shape),
                  resident((1, d)), resident((1, d))],
        out_specs=pl.BlockSpec((tm, d), lambda i: (i, 0)),
        out_shape=jax.ShapeDtypeStruct((t, d), F32),
        compiler_params=pltpu.CompilerParams(
            dimension_semantics=("arbitrary",),
            vmem_limit_bytes=VMEM_LIMIT_BYTES),
        name="outproj",
    )(oa, y, zs, x2d, p2d, wo, wg, wp, cnw.reshape(1, d), fw.reshape(1, d))


def kernel(x, p, norm_w, w_in, lb_theta, hgrn_norm_w, conv_w, conv_norm_w, w_out, w_ple,
           w_ple_gate, final_norm_w):
    batch, seq, d = x.shape
    depth = p.shape[0]
    assert depth == 1 and lb_theta.shape[1] == 2
    t = batch * seq
    n_heads = d // LANES

    x2d = x.reshape(t, d)
    hn = _rmsnorm(x2d, norm_w[0])
    w = w_in[0]

    theta2 = jnp.transpose(lb_theta, (1, 0, 2)).reshape(2, 2 * d)
    hi, lo, k, sil, v = _inproj_hgrn(hn, w, theta2, 2 * d, (0 * d, 4 * d), 1 * d, d)
    o_a = _hgrn(sil, v, hi, lo, k, hgrn_norm_w[0].reshape(n_heads, LANES), batch, seq, n_heads)

    y_b, zs_b, wo, wg, wp = _convproj(hn, w, conv_w[0], 5 * d, seq, d,
                                      (w_out[0], w_ple_gate[0], w_ple[0]))

    out = _out(o_a, y_b, zs_b, x2d, p[0].reshape(t, -1), wo, wg, wp,
               conv_norm_w[0], final_norm_w)
    return out.reshape(batch, seq, d)
```

```python
import jax
import jax.numpy as jnp
from jax import lax
from jax.experimental import pallas as pl
from jax.experimental.pallas import tpu as pltpu

EPS = 1e-6
LOG2E = 1.4426950408889634
LANES = 128
BF16_SUBLANES = 16
CHUNK = 64
HEADS_PER_STEP = 2
VMEM_CAPACITY_BYTES = 64 * 1024 * 1024
VMEM_LIMIT_BYTES = VMEM_CAPACITY_BYTES - 4 * 1024 * 1024

F32 = jnp.float32
BF16 = jnp.bfloat16


def _sigmoid(x):
    return 0.5 * jnp.tanh(0.5 * x) + 0.5


def _dot(a, b, dims=((1,), (0,))):
    return lax.dot_general(a, b, (dims, ((), ())), preferred_element_type=F32)


def _rmsnorm_kernel(x_ref, w_ref, o_ref):
    x = x_ref[...]
    ms = jnp.mean(x * x, axis=-1, keepdims=True)
    o_ref[...] = (x * lax.rsqrt(ms + EPS) * w_ref[...]).astype(o_ref.dtype)


def _rmsnorm(x2d, w, tm=1024):
    t, d = x2d.shape
    return pl.pallas_call(
        _rmsnorm_kernel,
        grid=(t // tm,),
        in_specs=[pl.BlockSpec((tm, d), lambda i: (i, 0)),
                  pl.BlockSpec((1, d), lambda i: (0, 0))],
        out_specs=pl.BlockSpec((tm, d), lambda i: (i, 0)),
        out_shape=jax.ShapeDtypeStruct((t, d), BF16),
        name="rmsnorm",
    )(x2d, w.reshape(1, d))


def _store_heads(o_ref, val):
    for g in range(val.shape[1] // LANES):
        o_ref[g] = val[:, g * LANES:(g + 1) * LANES].astype(o_ref.dtype)


def _inproj_hgrn_kernel(h_ref, wg_ref, ws_ref, wv_ref, theta_ref, hi_ref, lo_ref, k_ref, s_ref,
                        v_ref, wbf_ref, acc_ref):
    tg = wg_ref.shape[1]
    ts = ws_ref.shape[1]

    @pl.when(pl.program_id(1) == 0)
    def _():
        wbf_ref[:, 0:tg] = wg_ref[...].astype(BF16)
        wbf_ref[:, tg:tg + ts] = ws_ref[...].astype(BF16)
        wbf_ref[:, tg + ts:] = wv_ref[...].astype(BF16)

    h = h_ref[...]
    acc_ref[...] = jnp.dot(h, wbf_ref[:, 0:tg], preferred_element_type=F32)
    t0 = theta_ref[0:1, :]
    t1 = theta_ref[1:2, :]
    m = jnp.maximum(t0, t1)
    e0 = jnp.exp(t0 - m)
    e1 = jnp.exp(t1 - m)
    lb = e0 / (e0 + e1)
    f = 0.5 * (1.0 + lb) + (0.5 * (1.0 - lb)) * jnp.tanh(0.5 * acc_ref[...])
    lf = jnp.log(f)
    hi = lf.astype(BF16)
    _store_heads(hi_ref, hi)
    _store_heads(lo_ref, lf - hi.astype(F32))
    _store_heads(k_ref, 1.0 - f)
    acc = jnp.dot(h, wbf_ref[:, tg:tg + ts], preferred_element_type=F32)
    _store_heads(s_ref, acc * _sigmoid(acc))
    _store_heads(v_ref, jnp.dot(h, wbf_ref[:, tg + ts:], preferred_element_type=F32))


def _inproj_hgrn(hn, w, theta2, gate_col_start, silu_col_starts, value_col_start, n_cols,
                 tm=1024, tg=512):
    t, d = hn.shape
    n_blocks = 2 * n_cols // tg
    half = n_blocks // 2
    tv = n_cols // n_blocks
    g0 = gate_col_start // tg
    s0, s1 = (cs // tg for cs in silu_col_starts)
    v0 = value_col_start // tv

    def out(width, n_slices):
        return (pl.BlockSpec((width // LANES, tm, LANES), lambda c, i: (c, i, 0)),
                jax.ShapeDtypeStruct((n_slices * n_cols // LANES, t, LANES), BF16))

    outs = [out(tg, 2)] * 4 + [out(tv, 1)]
    n_row_tiles = t // tm

    def ahead(c, i, steps):
        return jnp.minimum(c + (i >= n_row_tiles - steps).astype(jnp.int32), n_blocks - 1)

    def silu_block(c):
        return jnp.where(c < half, s0 + c, s1 + c - half)

    return pl.pallas_call(
        _inproj_hgrn_kernel,
        grid=(n_blocks, n_row_tiles),
        in_specs=[pl.BlockSpec((tm, d), lambda c, i: (i, 0)),
                  pl.BlockSpec((d, tg), lambda c, i: (0, g0 + ahead(c, i, 3))),
                  pl.BlockSpec((d, tg), lambda c, i: (0, silu_block(ahead(c, i, 2)))),
                  pl.BlockSpec((d, tv), lambda c, i: (0, v0 + ahead(c, i, 1))),
                  pl.BlockSpec((theta2.shape[0], tg), lambda c, i: (0, c))],
        out_specs=[o[0] for o in outs],
        out_shape=[o[1] for o in outs],
        scratch_shapes=[pltpu.VMEM((d, 2 * tg + tv), BF16), pltpu.VMEM((tm, tg), F32)],
        compiler_params=pltpu.CompilerParams(
            dimension_semantics=("arbitrary", "arbitrary"),
            vmem_limit_bytes=VMEM_LIMIT_BYTES),
        name="inproj_hgrn",
    )(hn, w, w, w, theta2)


def _convproj_kernel(h_ref, wb_ref, wc_ref, wh_ref, wz_ref, cw_ref, wo_ref, wg_ref, wp_ref,
                     y_ref, zs_ref, wo_bf_ref, wg_bf_ref, wp_bf_ref, wbf_ref):
    seq = h_ref.shape[0]
    wo_bf_ref[...] = wo_ref[...].astype(BF16)
    wg_bf_ref[...] = wg_ref[...].astype(BF16)
    wp_bf_ref[...] = wp_ref[...].astype(BF16)

    @pl.when(pl.program_id(1) == 0)
    def _():
        for k, w_ref in enumerate((wb_ref, wc_ref, wh_ref, wz_ref)):
            wbf_ref[k] = w_ref[...].astype(BF16)

    h = h_ref[...]
    cg = jnp.dot(h, wbf_ref[1], preferred_element_type=F32)
    hb = jnp.dot(h, wbf_ref[2], preferred_element_type=F32)
    u = cg * hb
    row = lax.broadcasted_iota(jnp.int32, u.shape, 0)
    u_prev = jnp.where(row == 0, 0.0, pltpu.roll(u, 1, axis=0))
    u_next = jnp.where(row == seq - 1, 0.0, pltpu.roll(u, seq - 1, axis=0))
    conv = cw_ref[0:1, :] * u_prev + cw_ref[1:2, :] * u + cw_ref[2:3, :] * u_next
    zb = jnp.dot(h, wbf_ref[3], preferred_element_type=F32)
    zs_ref[...] = (zb * _sigmoid(zb)).astype(zs_ref.dtype)
    bg = jnp.dot(h, wbf_ref[0], preferred_element_type=F32)
    y_ref[...] = (bg * conv).astype(y_ref.dtype)


def _convproj(hn, w, cw, col_start, seq, n_ch, out_weights, tnc=256):
    t, d = hn.shape
    batch = t // seq
    blk0 = col_start // tnc
    per_slice = n_ch // tnc
    n_steps = per_slice * batch

    def w_spec(k):
        return pl.BlockSpec((d, tnc), lambda c, b, k=k: (0, blk0 + k * per_slice + c))

    def slab_spec(a):
        rows = max(a.shape[0] // n_steps, BF16_SUBLANES)
        last = a.shape[0] // rows - 1
        return pl.BlockSpec((rows, a.shape[1]),
                            lambda c, b: (jnp.minimum(c * batch + b, last), 0))

    slabs = [slab_spec(a) for a in out_weights]
    out_spec = pl.BlockSpec((seq, tnc), lambda c, b: (b, c))
    out_sds = jax.ShapeDtypeStruct((t, n_ch), BF16)
    return pl.pallas_call(
        _convproj_kernel,
        grid=(per_slice, batch),
        in_specs=[pl.BlockSpec((seq, d), lambda c, b: (b, 0)),
                  w_spec(0), w_spec(1), w_spec(2), w_spec(3),
                  pl.BlockSpec((cw.shape[0], tnc), lambda c, b: (0, c))] + slabs,
        out_specs=[out_spec, out_spec] + slabs,
        out_shape=[out_sds, out_sds] + [jax.ShapeDtypeStruct(a.shape, BF16)
                                        for a in out_weights],
        scratch_shapes=[pltpu.VMEM((4, d, tnc), BF16)],
        compiler_params=pltpu.CompilerParams(
            dimension_semantics=("arbitrary", "arbitrary"),
            vmem_limit_bytes=VMEM_LIMIT_BYTES),
        name="convproj",
    )(hn, w, w, w, w, cw, *out_weights)


def _hgrn_kernel(qs_ref, v_ref, hif_ref, lof_ref, kf_ref, hib_ref, lob_ref, kb_ref, zs_ref,
                 nw_ref, o_ref, acc_ref, qin_ref, kv_ref, dec_ref, sprev_ref):
    seq = qs_ref.shape[1]
    c = CHUNK
    n_chunks = seq // c
    head0 = pl.program_id(1) * HEADS_PER_STEP
    mid = c // 2 - 1

    row = lax.broadcasted_iota(jnp.int32, (c, c), 0)
    col = lax.broadcasted_iota(jnp.int32, (c, c), 1)
    masks = (col <= row, col >= row)
    row2 = lax.broadcasted_iota(jnp.int32, (c, 2 * c), 0)
    col2 = lax.broadcasted_iota(jnp.int32, (c, 2 * c), 1) % c
    tris = ((col2 <= row2).astype(BF16), (col2 >= row2).astype(BF16))
    hi_refs = (hif_ref, hib_ref)
    lo_refs = (lof_ref, lob_ref)
    k_refs = (kf_ref, kb_ref)
    mid_rows = (mid, c - 1 - mid)
    last_rows = (c - 1, 0)

    def rows(j):
        return slice(j * c, (j + 1) * c)

    def intra(hh):
        qs = qs_ref[hh].astype(F32)
        v = v_ref[hh]
        ks, bs = [], []
        for d in range(2):
            hi = hi_refs[d][hh]
            lo = lo_refs[d][hh]
            ks.append(k_refs[d][hh].astype(F32))
            bs.append([_dot(tris[d], jnp.concatenate([hi[rows(j)], lo[rows(j)]], axis=0))
                       for j in range(n_chunks)])
        ss = ([], [])
        for j in range(n_chunks):
            for d in range(2):
                b = bs[d][j]
                b_mid = b[mid_rows[d]:mid_rows[d] + 1, :]
                b_last = b[last_rows[d]:last_rows[d] + 1, :]
                rel = b - b_mid
                q_rel = qs[rows(j)] * jnp.exp2(rel * LOG2E)
                k_rel = ks[d][rows(j)] * jnp.exp2(rel * -LOG2E)
                ss[d].append(_dot(q_rel.astype(BF16), k_rel.astype(BF16), ((1,), (1,))))
                q_in = q_rel * jnp.exp(b_mid)
                k_st = k_rel * jnp.exp(b_last - b_mid)
                qin_ref[hh, d, rows(j), :] = q_in.astype(BF16)
                kv_ref[hh, d, j] = _dot(v[rows(j)], k_st.astype(BF16), ((0,), (0,)))
                dec_ref[hh, d, j] = jnp.broadcast_to(jnp.exp(b_last), (8, LANES))
        for j in range(n_chunks):
            p = jnp.where(masks[0], ss[0][j], 0.0) + jnp.where(masks[1], ss[1][j], 0.0)
            acc_ref[hh, rows(j), :] = _dot(p.astype(BF16), v[rows(j)])

    def scan(hh):
        st_f = jnp.zeros((LANES, LANES), F32)
        st_b = st_f
        for i in range(n_chunks):
            nb = n_chunks - 1 - i
            sprev_ref[hh, i, :, 0:LANES] = st_f.astype(BF16)
            sprev_ref[hh, nb, :, LANES:2 * LANES] = st_b.astype(BF16)
            st_f = st_f * dec_ref[hh, 0, i][0:1, :] + kv_ref[hh, 0, i]
            st_b = st_b * dec_ref[hh, 1, nb][0:1, :] + kv_ref[hh, 1, nb]

    def finish(hh):
        inter = []
        for j in range(n_chunks):
            qin = jnp.concatenate([qin_ref[hh, 0, rows(j), :], qin_ref[hh, 1, rows(j), :]],
                                  axis=1)
            inter.append(_dot(qin, sprev_ref[hh, j], ((1,), (1,))))
        o = acc_ref[hh] + jnp.concatenate(inter, axis=0)
        ms = jnp.mean(o * o, axis=-1, keepdims=True)
        nw = nw_ref[pl.ds(head0 + hh, 1), :]
        o_ref[:, hh * LANES:(hh + 1) * LANES] = (
            o * lax.rsqrt(ms + EPS) * nw * zs_ref[hh].astype(F32)).astype(o_ref.dtype)

    for stage in (intra, scan, finish):
        for hh in range(HEADS_PER_STEP):
            stage(hh)


def _hgrn(sil, v, hi, lo, k, nw, batch, seq, n_heads):
    t = batch * seq
    n_chunks = seq // CHUNK
    hps = HEADS_PER_STEP
    groups = n_heads // hps

    def spec(second_half):
        return pl.BlockSpec((hps, seq, LANES), lambda b, g: (second_half * groups + g, b, 0))

    fwd, bwd = spec(0), spec(1)
    return pl.pallas_call(
        _hgrn_kernel,
        grid=(batch, n_heads // hps),
        in_specs=[fwd, fwd, fwd, fwd, fwd, bwd, bwd, bwd, bwd,
                  pl.BlockSpec(nw.shape, lambda b, g: (0, 0))],
        out_specs=pl.BlockSpec((seq, hps * LANES), lambda b, g: (b, g)),
        out_shape=jax.ShapeDtypeStruct((t, n_heads * LANES), BF16),
        scratch_shapes=[
            pltpu.VMEM((hps, seq, LANES), F32),
            pltpu.VMEM((hps, 2, seq, LANES), BF16),
            pltpu.VMEM((hps, 2, n_chunks, LANES, LANES), F32),
            pltpu.VMEM((hps, 2, n_chunks, 8, LANES), F32),
            pltpu.VMEM((hps, n_chunks, LANES, 2 * LANES), BF16),
        ],
        compiler_params=pltpu.CompilerParams(
            dimension_semantics=("arbitrary", "arbitrary"),
            vmem_limit_bytes=VMEM_LIMIT_BYTES),
        name="hgrn2",
    )(sil, v, hi, lo, k, hi, lo, k, sil, nw)


def _out_kernel(oa_ref, y_ref, zs_ref, x_ref, p_ref, wo_ref, wg_ref, wp_ref, cnw_ref, fw_ref,
                o_ref):
    d_a = oa_ref.shape[1]
    y = y_ref[...].astype(F32)
    ms_b = jnp.mean(y * y, axis=-1, keepdims=True)
    ob = (y * zs_ref[...].astype(F32) * (lax.rsqrt(ms_b + EPS) * cnw_ref[...])).astype(BF16)
    h1 = (x_ref[...]
          + jnp.dot(oa_ref[...], wo_ref[0:d_a, :], preferred_element_type=F32)
          + jnp.dot(ob, wo_ref[d_a:, :], preferred_element_type=F32))
    gate = _sigmoid(jnp.dot(h1.astype(BF16), wg_ref[...], preferred_element_type=F32))
    ple = jnp.dot(p_ref[...].astype(BF16), wp_ref[...], preferred_element_type=F32)
    h2 = h1 + ple * gate
    ms = jnp.mean(h2 * h2, axis=-1, keepdims=True)
    o_ref[...] = h2 * lax.rsqrt(ms + EPS) * fw_ref[...]


def _out(oa, y, zs, x2d, p2d, wo, wg, wp, cnw, fw, tm=256):
    t, d = x2d.shape

    def resident(shape):
        return pl.BlockSpec(shape, lambda i: (0, 0), pipeline_mode=pl.Buffered(1))

    def rows(a):
        return pl.BlockSpec((tm, a.shape[1]), lambda i: (i, 0))

    return pl.pallas_call(
        _out_kernel,
        grid=(t // tm,),
        in_specs=[rows(oa), rows(y), rows(zs), rows(x2d), rows(p2d),
                  resident(wo.shape), resident(wg.shape), resident(wp.shape),
                  resident((1, d)), resident((1, d))],
        out_specs=pl.BlockSpec((tm, d), lambda i: (i, 0)),
        out_shape=jax.ShapeDtypeStruct((t, d), F32),
        compiler_params=pltpu.CompilerParams(
            dimension_semantics=("arbitrary",),
            vmem_limit_bytes=VMEM_LIMIT_BYTES),
        name="outproj",
    )(oa, y, zs, x2d, p2d, wo, wg, wp, cnw.reshape(1, d), fw.reshape(1, d))


def kernel(x, p, norm_w, w_in, lb_theta, hgrn_norm_w, conv_w, conv_norm_w, w_out, w_ple,
           w_ple_gate, final_norm_w):
    batch, seq, d = x.shape
    depth = p.shape[0]
    assert depth == 1 and lb_theta.shape[1] == 2
    t = batch * seq
    n_heads = d // LANES

    x2d = x.reshape(t, d)
    hn = _rmsnorm(x2d, norm_w[0])
    w = w_in[0]

    theta2 = jnp.transpose(lb_theta, (1, 0, 2)).reshape(2, 2 * d)
    hi, lo, k, sil, v = _inproj_hgrn(hn, w, theta2, 2 * d, (0 * d, 4 * d), 1 * d, d)
    o_a = _hgrn(sil, v, hi, lo, k, hgrn_norm_w[0].reshape(n_heads, LANES), batch, seq, n_heads)

    y_b, zs_b, wo, wg, wp = _convproj(hn, w, conv_w[0], 5 * d, seq, d,
                                      (w_out[0], w_ple_gate[0], w_ple[0]))

    out = _out(o_a, y_b, zs_b, x2d, p[0].reshape(t, -1), wo, wg, wp,
               conv_norm_w[0], final_norm_w)
    return out.reshape(batch, seq, d)
```
